```python
import jax
import jax.numpy as jnp
from jax import lax
import numpy as np

D_MODEL = 1024
BATCH = 2
SEQ = 16384
DEPTH = 2
DEC_BATCH = 8
DEC_SEQ = 64
PAST_LEN = 1024

CHUNK = 64
HA = 4
HDA = 64
DA = HA * HDA
SB_BLOCK = 128
HB = 4
HDB = 64
DB = HB * HDB
BAND_CHUNKS = 8
REL_MAX = 128
HC = 4
HDC = 128
DC = HC * HDC
CONV_W = 4
D_MIX = DA + DB + DC
N_GROUPS = 4
E_PER_GROUP = 4
N_EXPERTS = N_GROUPS * E_PER_GROUP
TOP_K = 2
D_EXPERT = 512
DEEPNORM_ALPHA = (2 * DEPTH) ** 0.25
DEEPNORM_BETA = (8 * DEPTH) ** -0.25
LN_EPS = 1e-5
IN_SIZES = (DA, DA, DA, DB, DB, DB, DC, DC, DC, DC, HC, HC)
IN_COLS = sum(IN_SIZES)
IN_SPLITS = tuple(int(s) for s in np.cumsum(IN_SIZES)[:-1])

kernel_name = 'hybrid_stream_sb_band_mlstm_hmoe'


def layer_norm(x, g, b):
    xf = x.astype(jnp.float32)
    mu = jnp.mean(xf, axis=-1, keepdims=True)
    var = jnp.mean(jnp.square(xf - mu), axis=-1, keepdims=True)
    return ((xf - mu) * lax.rsqrt(var + LN_EPS) * g + b).astype(x.dtype)


def stick_breaking(q, k, v, q_pos, k_pos):
    z = jnp.einsum('bqhd,bkhd->bhqk', q, k).astype(jnp.float32) * (HDA ** -0.5)
    mask = k_pos[None, :] < q_pos[:, None]
    log_stay = jnp.where(mask, jax.nn.log_sigmoid(-z), 0.0)
    after = lax.cumsum(log_stay, axis=3, reverse=True) - log_stay
    w = jnp.where(mask, jnp.exp(jax.nn.log_sigmoid(z) + after), 0.0)
    return jnp.einsum('bhqk,bkhd->bqhd', w.astype(v.dtype), v)


def sb_attention(q, k, v, past):
    bn, t = q.shape[:2]
    blk = SB_BLOCK if t % SB_BLOCK == 0 else t
    nb = t // blk
    k_pos = jnp.arange(k.shape[1])
    q_blocks = jnp.moveaxis(q.reshape(bn, nb, blk, HA, HDA), 1, 0)
    p_blocks = (past + jnp.arange(t)).reshape(nb, blk)
    out = lax.map(lambda qp: stick_breaking(qp[0], k, v, qp[1], k_pos), (q_blocks, p_blocks))
    return jnp.moveaxis(out, 0, 1).reshape(bn, t, HA, HDA)


def gather_rel_bias(rel_table, rel):
    return rel_table[:, jnp.clip(rel, -REL_MAX, REL_MAX) + REL_MAX].astype(jnp.float32)


def band_softmax(q, k, v, bias, valid):
    s = jnp.einsum('...qhd,...khd->...hqk', q, k).astype(jnp.float32) * (HDB ** -0.5) + bias
    p = jax.nn.softmax(jnp.where(valid, s, -jnp.inf), axis=-1)
    return jnp.einsum('...hqk,...khd->...qhd', p.astype(v.dtype), v)


def band_attention_prompt(q, k, v, rel_table):
    bn, t = q.shape[:2]
    nc = t // CHUNK
    width = (BAND_CHUNKS + 1) * CHUNK

    def band(a):
        a = a.reshape(bn, nc, CHUNK, HB, HDB)
        ap = jnp.pad(a, ((0, 0), (BAND_CHUNKS, 0), (0, 0), (0, 0), (0, 0)))
        return jnp.concatenate([ap[:, o:o + nc] for o in range(BAND_CHUNKS + 1)], axis=2)

    qc = q.reshape(bn, nc, CHUNK, HB, HDB)
    kk = jnp.arange(width) - BAND_CHUNKS * CHUNK
    qi = jnp.arange(CHUNK)
    bias = gather_rel_bias(rel_table, qi[:, None] - kk[None, :])
    valid = (jnp.arange(nc)[:, None] * CHUNK + kk[None, :]) >= 0
    out = band_softmax(qc, band(k), band(v), bias, valid[None, :, None, None, :])
    return out.reshape(bn, t, HB, HDB)


def band_attention_sample(q, k, v, hist_k, hist_v, past, rel_table):
    t = q.shape[1]
    pb = hist_k.shape[1]
    kf = jnp.concatenate([hist_k.astype(k.dtype), k], axis=1)
    vf = jnp.concatenate([hist_v.astype(v.dtype), v], axis=1)
    q_pos = past + jnp.arange(t)
    k_pos = past - pb + jnp.arange(pb + t)
    qch = q_pos[:, None] // CHUNK
    kch = k_pos[None, :] // CHUNK
    valid = (kch >= qch - BAND_CHUNKS) & (kch <= qch)
    bias = gather_rel_bias(rel_table, q_pos[:, None] - k_pos[None, :])
    return band_softmax(q, kf, vf, bias, valid)


def mlstm_chunk(carry, xs):
    c_prev, n_prev, m_prev = carry
    q, k, v, li, lf = xs
    length = q.shape[2]
    b = jnp.cumsum(lf, axis=-1)
    causal = jnp.tril(jnp.ones((length, length), dtype=bool))
    dmat = jnp.where(causal, b[..., :, None] - b[..., None, :] + li[..., None, :], -jnp.inf)
    inter = b + m_prev[..., None]
    m_t = jnp.maximum(inter, jnp.max(dmat, axis=-1))
    w_intra = jnp.exp(dmat - m_t[..., None])
    w_inter = jnp.exp(inter - m_t)
    s = jnp.einsum('bhtd,bhsd->bhts', q, k) * w_intra
    num = w_inter[..., None] * jnp.einsum('bhtd,bhde->bhte', q, c_prev) + jnp.einsum('bhts,bhse->bhte', s, v)
    den = w_inter * jnp.einsum('bhtd,bhd->bht', q, n_prev) + jnp.sum(s, axis=-1)
    h = num / jnp.maximum(jnp.abs(den), jnp.exp(-m_t))[..., None]
    m_new = m_t[..., -1]
    decay = jnp.exp(inter[..., -1] - m_new)
    w_k = jnp.exp(b[..., -1:] - b + li - m_new[..., None])
    c_new = decay[..., None, None] * c_prev + jnp.einsum('bhs,bhsd,bhse->bhde', w_k, k, v)
    n_new = decay[..., None] * n_prev + jnp.einsum('bhs,bhsd->bhd', w_k, k)
    return (c_new, n_new, m_new), h


def mlstm_mixer(qc, kc, vc, oc, ic, fc, conv_w, conv_b, b_if, norm_g, c0, n0, m0, conv0):
    bn, t, _ = qc.shape
    u = jnp.concatenate([qc, kc], axis=-1)
    up = jnp.concatenate([conv0.astype(u.dtype), u], axis=1)
    y = sum((up[:, w:w + t] * conv_w[w] for w in range(CONV_W)), conv_b)
    q, k = jnp.split(jax.nn.silu(y), 2, axis=-1)

    def to_heads(a):
        return a.reshape(bn, t, HC, HDC).transpose(0, 2, 1, 3).astype(jnp.float32)

    q = to_heads(q)
    k = to_heads(k) * (HDC ** -0.5)
    v = to_heads(vc)
    b_if = b_if.astype(jnp.float32)
    li = (ic.astype(jnp.float32) + b_if[:HC]).transpose(0, 2, 1)
    lf = jax.nn.log_sigmoid(fc.astype(jnp.float32) + b_if[HC:]).transpose(0, 2, 1)
    length = CHUNK if t % CHUNK == 0 else t
    nc = t // length

    def chunks(a):
        return jnp.moveaxis(a.reshape(a.shape[:2] + (nc, length) + a.shape[3:]), 2, 0)

    carry0 = (c0.astype(jnp.float32), n0.astype(jnp.float32), m0.astype(jnp.float32))
    (c1, n1, m1), h = lax.scan(mlstm_chunk, carry0, (chunks(q), chunks(k), chunks(v), chunks(li), chunks(lf)))
    h = jnp.moveaxis(h, 0, 2).reshape(bn, HC, t, HDC).transpose(0, 2, 1, 3)
    hm = jnp.mean(h, axis=-1, keepdims=True)
    hv = jnp.mean(jnp.square(h - hm), axis=-1, keepdims=True)
    hn = (h - hm) * lax.rsqrt(hv + LN_EPS) * norm_g
    o = jax.nn.sigmoid(oc.astype(jnp.float32)).reshape(bn, t, HC, HDC)
    out = (o * hn).reshape(bn, t, DC).astype(qc.dtype)
    return out, (c1, n1, m1, up[:, -(CONV_W - 1):])


def token_mixer(x, w_in, b_if, conv_w, conv_b, rel_table, c_norm_g, w_out, cache):
    bn, t, _ = x.shape
    z = x @ w_in
    qa, ka, va, qb, kb, vb, qc, kc, vc, oc, ic, fc = jnp.split(z, IN_SPLITS, axis=-1)
    qa, ka, va = (a.reshape(bn, t, HA, HDA) for a in (qa, ka, va))
    qb, kb, vb = (a.reshape(bn, t, HB, HDB) for a in (qb, kb, vb))
    if cache is None:
        a_out = sb_attention(qa, ka, va, 0)
        b_out = band_attention_prompt(qb, kb, vb, rel_table)
        c0 = jnp.zeros((bn, HC, HDC, HDC), jnp.float32)
        n0 = jnp.zeros((bn, HC, HDC), jnp.float32)
        m0 = jnp.zeros((bn, HC), jnp.float32)
        conv0 = jnp.zeros((bn, CONV_W - 1, 2 * DC), x.dtype)
    else:
        a_k, a_v, b_k, b_v, c0, n0, m0, conv0 = cache
        past = a_k.shape[1]
        a_out = sb_attention(qa, jnp.concatenate([a_k.astype(ka.dtype), ka], axis=1),
                             jnp.concatenate([a_v.astype(va.dtype), va], axis=1), past)
        b_out = band_attention_sample(qb, kb, vb, b_k, b_v, past, rel_table)
    c_out, (c1, n1, m1, conv1) = mlstm_mixer(qc, kc, vc, oc, ic, fc, conv_w, conv_b, b_if, c_norm_g,
                                             c0, n0, m0, conv0)
    mixed = jnp.concatenate([a_out.reshape(bn, t, DA), b_out.reshape(bn, t, DB), c_out], axis=-1)
    return mixed @ w_out, (ka, va, kb, vb, c1, n1, m1, conv1)


def hier_moe(x, router_g_w, router_g_b, router_e_w, router_e_b, w_gate, w_up, w_down):
    shp = x.shape
    xt = x.reshape(-1, shp[-1])
    lg = (xt @ router_g_w).astype(jnp.float32) + router_g_b
    g_sel = jnp.argmax(lg, axis=-1)
    p_g = jnp.take_along_axis(jax.nn.softmax(lg, axis=-1), g_sel[:, None], axis=-1)
    le = ((xt @ router_e_w).astype(jnp.float32) + router_e_b).reshape(-1, N_GROUPS, E_PER_GROUP)
    le = jnp.take_along_axis(le, g_sel[:, None, None], axis=1)[:, 0]
    top_v, top_i = lax.top_k(le, TOP_K)
    gate = p_g * jax.nn.softmax(top_v, axis=-1)
    eid = g_sel[:, None] * E_PER_GROUP + top_i
    comb = jnp.sum(jax.nn.one_hot(eid, N_EXPERTS, dtype=jnp.float32) * gate[..., None], axis=1)
    y = jnp.zeros(xt.shape, jnp.float32)
    for e in range(N_EXPERTS):
        h = jax.nn.silu(xt @ w_gate[e]) * (xt @ w_up[e])
        y = y + comb[:, e:e + 1] * (h @ w_down[e])
    return y.astype(x.dtype).reshape(shp)


def setup_inputs(seed: int = 0) -> dict:
    key = jax.random.key(seed)
    ks = jax.random.split(key, 32)

    def nrm(k, shape, scale):
        return jax.random.normal(k, shape, jnp.float32) * scale

    past_b = min(BAND_CHUNKS * CHUNK, PAST_LEN)
    b_if = jnp.concatenate([nrm(ks[11], (DEPTH, HC), 0.1),
                            jnp.linspace(3.0, 6.0, HC, dtype=jnp.float32)[None, :] + nrm(ks[12], (DEPTH, HC), 0.1)],
                           axis=-1)
    return {
        'x_prompt': nrm(ks[0], (BATCH, SEQ, D_MODEL), 1.0),
        'x_sample': nrm(ks[1], (DEC_BATCH, DEC_SEQ, D_MODEL), 1.0),
        'cache_a_k': nrm(ks[2], (DEPTH, DEC_BATCH, PAST_LEN, HA, HDA), 1.0),
        'cache_a_v': nrm(ks[3], (DEPTH, DEC_BATCH, PAST_LEN, HA, HDA), 1.0),
        'cache_b_k': nrm(ks[4], (DEPTH, DEC_BATCH, past_b, HB, HDB), 1.0),
        'cache_b_v': nrm(ks[5], (DEPTH, DEC_BATCH, past_b, HB, HDB), 1.0),
        'state_c_C': nrm(ks[6], (DEPTH, DEC_BATCH, HC, HDC, HDC), 0.3),
        'state_c_n': nrm(ks[7], (DEPTH, DEC_BATCH, HC, HDC), 0.3),
        'state_c_m': nrm(ks[8], (DEPTH, DEC_BATCH, HC), 1.0),
        'state_c_conv': nrm(ks[9], (DEPTH, DEC_BATCH, CONV_W - 1, 2 * DC), 1.0),
        'w_in': nrm(ks[10], (DEPTH, D_MODEL, IN_COLS), D_MODEL ** -0.5),
        'b_if': b_if,
        'conv_w': nrm(ks[13], (DEPTH, CONV_W, 2 * DC), CONV_W ** -0.5),
        'conv_b': nrm(ks[14], (DEPTH, 2 * DC), 0.02),
        'rel_table': nrm(ks[15], (DEPTH, HB, 2 * REL_MAX + 1), 0.5),
        'c_norm_g': 1.0 + nrm(ks[16], (DEPTH, HC, HDC), 0.02),
        'w_out': nrm(ks[17], (DEPTH, D_MIX, D_MODEL), D_MIX ** -0.5 * DEEPNORM_BETA),
        'ln1_g': 1.0 + nrm(ks[18], (DEPTH, D_MODEL), 0.02),
        'ln1_b': nrm(ks[19], (DEPTH, D_MODEL), 0.02),
        'router_g_w': nrm(ks[20], (DEPTH, D_MODEL, N_GROUPS), D_MODEL ** -0.5),
        'router_g_b': nrm(ks[21], (DEPTH, N_GROUPS), 0.01),
        'router_e_w': nrm(ks[22], (DEPTH, D_MODEL, N_EXPERTS), D_MODEL ** -0.5),
        'router_e_b': nrm(ks[23], (DEPTH, N_EXPERTS), 0.01),
        'exp_w_gate': nrm(ks[24], (DEPTH, N_EXPERTS, D_MODEL, D_EXPERT), D_MODEL ** -0.5),
        'exp_w_up': nrm(ks[25], (DEPTH, N_EXPERTS, D_MODEL, D_EXPERT), D_MODEL ** -0.5),
        'exp_w_down': nrm(ks[26], (DEPTH, N_EXPERTS, D_EXPERT, D_MODEL), D_EXPERT ** -0.5 * DEEPNORM_BETA),
        'ln2_g': 1.0 + nrm(ks[27], (DEPTH, D_MODEL), 0.02),
        'ln2_b': nrm(ks[28], (DEPTH, D_MODEL), 0.02),
    }


def reference(x_prompt, x_sample, cache_a_k, cache_a_v, cache_b_k, cache_b_v, state_c_C, state_c_n,
              state_c_m, state_c_conv, w_in, b_if, conv_w, conv_b, rel_table, c_norm_g, w_out, ln1_g, ln1_b,
              router_g_w, router_g_b, router_e_w, router_e_b, exp_w_gate, exp_w_up, exp_w_down, ln2_g, ln2_b):
    band_keep = cache_b_k.shape[2]
    xp = x_prompt
    xs = x_sample
    prompt_states = []
    sample_states = []
    for l in range(DEPTH):
        mix_w = (w_in[l], b_if[l], conv_w[l], conv_b[l], rel_table[l], c_norm_g[l], w_out[l])
        ffn_w = (router_g_w[l], router_g_b[l], router_e_w[l], router_e_b[l],
                 exp_w_gate[l], exp_w_up[l], exp_w_down[l])
        mp, sp = token_mixer(xp, *mix_w, None)
        xp = layer_norm(DEEPNORM_ALPHA * xp + mp, ln1_g[l], ln1_b[l])
        xp = layer_norm(DEEPNORM_ALPHA * xp + hier_moe(xp, *ffn_w), ln2_g[l], ln2_b[l])
        ak, av, bk, bv, cc, cn, cm, cconv = sp
        prompt_states.append((ak, av, bk[:, -band_keep:], bv[:, -band_keep:], cc, cn, cm, cconv))
        cache_l = (cache_a_k[l], cache_a_v[l], cache_b_k[l], cache_b_v[l],
                   state_c_C[l], state_c_n[l], state_c_m[l], state_c_conv[l])
        ms, ss = token_mixer(xs, *mix_w, cache_l)
        xs = layer_norm(DEEPNORM_ALPHA * xs + ms, ln1_g[l], ln1_b[l])
        xs = layer_norm(DEEPNORM_ALPHA * xs + hier_moe(xs, *ffn_w), ln2_g[l], ln2_b[l])
        sample_states.append(ss)
    p_a_k, p_a_v, p_b_k, p_b_v, p_c_C, p_c_n, p_c_m, p_c_conv = [jnp.stack(s) for s in zip(*prompt_states)]
    s_a_k, s_a_v, s_b_k, s_b_v, s_c_C, s_c_n, s_c_m, s_c_conv = [jnp.stack(s) for s in zip(*sample_states)]
    return (xp, xs, p_a_k, p_a_v, p_b_k, p_b_v, p_c_C, p_c_n, p_c_m, p_c_conv,
            s_a_k, s_a_v, s_b_k, s_b_v, s_c_C, s_c_n, s_c_m, s_c_conv)
```

```python
import functools

import jax
import jax.numpy as jnp
from jax import lax
from jax.experimental import pallas as pl
from jax.experimental.pallas import tpu as pltpu

F32 = jnp.float32
BF16 = jnp.bfloat16

D_MODEL = 1024
DEPTH = 2
CHUNK = 64
HA, HDA = 4, 64
HB, HDB = 4, 64
HC, HDC = 4, 128
DA, DB, DC = HA * HDA, HB * HDB, HC * HDC
BAND_CHUNKS = 8
BAND_W = (BAND_CHUNKS + 1) * CHUNK
REL_MAX = 128
CONV_W = 4
N_GROUPS = 4
E_PER_GROUP = 4
N_EXPERTS = N_GROUPS * E_PER_GROUP
D_EXPERT = 512
ALPHA = (2 * DEPTH) ** 0.25
LN_EPS = 1e-5

LANES = 128
Z_MAIN = 3 * DA + 3 * DB + 4 * DC
Z_COLS = Z_MAIN + LANES
SB_BLOCK = 128
SB_LOG_CUTOFF = -104.0
VMEM_LIMIT = 48 * 1024 * 1024


def _cparams(*sem):
    return pltpu.CompilerParams(dimension_semantics=sem, vmem_limit_bytes=VMEM_LIMIT)


def _split3(x):
    hi = x.astype(BF16)
    r1 = x - hi.astype(F32)
    mid = r1.astype(BF16)
    lo = (r1 - mid.astype(F32)).astype(BF16)
    return hi, mid, lo


def _dot(a, b):
    return jnp.dot(a, b, preferred_element_type=F32)


def _dot_nt(a, b):
    return lax.dot_general(a, b, (((1,), (1,)), ((), ())), preferred_element_type=F32)


def _dot_tn(a, b):
    return lax.dot_general(a, b, (((0,), (0,)), ((), ())), preferred_element_type=F32)


def _log_sigmoid_pair(z):
    t = jnp.log1p(jnp.exp(-jnp.abs(z)))
    return -(jnp.maximum(-z, 0.0) + t), -(jnp.maximum(z, 0.0) + t)


def _head_masks(rows, n_heads, width):
    lane = lax.broadcasted_iota(jnp.int32, (rows, n_heads * width), 1)
    return [(lane >= h * width) & (lane < (h + 1) * width) for h in range(n_heads)]


def _in_proj_kernel(x_ref, w_ref, z_ref, *, col_step):
    xb = x_ref[...].astype(BF16)
    for c0 in range(0, Z_COLS, col_step):
        c1 = min(c0 + col_step, Z_COLS)
        z_ref[:, c0:c1] = _dot(xb, w_ref[:, c0:c1])


def _in_proj(x2d, w_bf16, tm):
    n = x2d.shape[0]
    return pl.pallas_call(
        functools.partial(_in_proj_kernel, col_step=512),
        grid=(n // tm,),
        in_specs=[pl.BlockSpec((tm, D_MODEL), lambda i: (i, 0)),
                  pl.BlockSpec((D_MODEL, Z_COLS), lambda i: (0, 0))],
        out_specs=pl.BlockSpec((tm, Z_COLS), lambda i: (i, 0)),
        out_shape=jax.ShapeDtypeStruct((n, Z_COLS), F32),
        compiler_params=_cparams("parallel"),
        name="in_proj",
    )(x2d, w_bf16)


def _sb_kernel(q_ref, k_ref, v_ref, o_ref, carry_ref, acc_ref, *, past, bq, bk):
    i = pl.program_id(1)
    q = (q_ref[...] * (HDA ** -0.5)).astype(BF16)
    hmask = _head_masks(bq, HA, HDA)
    qh = [jnp.where(hmask[h], q, jnp.zeros_like(q)) for h in range(HA)]
    carry_ref[...] = jnp.zeros_like(carry_ref)
    acc_ref[...] = jnp.zeros_like(acc_ref)
    jj = lax.broadcasted_iota(jnp.int32, (bk, 2 * bk), 0)
    ss = lax.broadcasted_iota(jnp.int32, (bk, 2 * bk), 1)
    csum = jnp.where((ss >= bk) | (jj > ss), 1.0, 0.0).astype(BF16)
    q_pos = past + i * bq + lax.broadcasted_iota(jnp.int32, (bq, bk), 0)
    k_off = lax.broadcasted_iota(jnp.int32, (bq, bk), 1)
    kb_last = (past + (i + 1) * bq - 2) // bk

    def cond(state):
        kb, done = state
        return jnp.logical_and(kb >= 0, done == 0)

    def body(state):
        kb, _ = state
        start = pl.multiple_of(kb * bk, bk)
        kblk = k_ref[pl.ds(start, bk), :]
        vblk = v_ref[pl.ds(start, bk), :]
        mask = (start + k_off) < q_pos
        worst = None
        for h in range(HA):
            s = _dot_nt(qh[h], kblk)
            ls_pos, ls_neg = _log_sigmoid_pair(s)
            log_stay = jnp.where(mask, ls_neg, 0.0)
            hi, mid, lo = _split3(log_stay)
            cs = _dot(hi, csum) + _dot(mid, csum) + _dot(lo, csum)
            carry = carry_ref[h]
            w = jnp.where(mask, jnp.exp(ls_pos + carry + cs[:, :bk]), 0.0)
            pv = _dot(w.astype(BF16), vblk)
            acc_ref[...] += jnp.where(hmask[h], pv, 0.0)
            carry = carry + cs[:, bk:]
            carry_ref[h] = carry
            worst = carry if worst is None else jnp.maximum(worst, carry)
        done = (jnp.max(worst) < SB_LOG_CUTOFF).astype(jnp.int32)
        return kb - 1, done

    lax.while_loop(cond, body, (kb_last, jnp.int32(0)))
    o_ref[...] = acc_ref[...].astype(o_ref.dtype)


def _sb_attention(z, q_col, k_bf16, v_bf16, *, batch, tq, past, bq):
    bk = SB_BLOCK
    nq = tq // bq
    tk = k_bf16.shape[1]
    assert tk % bk == 0 and (past + tq - 2) // bk < tk // bk
    return pl.pallas_call(
        functools.partial(_sb_kernel, past=past, bq=bq, bk=bk),
        grid=(batch, nq),
        in_specs=[pl.BlockSpec((bq, DA), lambda b, i: (b * nq + i, q_col)),
                  pl.BlockSpec((None, tk, DA), lambda b, i: (b, 0, 0)),
                  pl.BlockSpec((None, tk, DA), lambda b, i: (b, 0, 0))],
        out_specs=pl.BlockSpec((bq, DA), lambda b, i: (b * nq + i, 0)),
        out_shape=jax.ShapeDtypeStruct((batch * tq, DA), BF16),
        scratch_shapes=[pltpu.VMEM((HA, bq, bk), F32), pltpu.VMEM((bq, DA), F32)],
        compiler_params=_cparams("parallel", "arbitrary"),
        name="sb_attention",
    )(z, k_bf16, v_bf16)


def _band_kernel(q_ref, k_ref, v_ref, bias_ref, o_ref, *, first_chunk):
    c = pl.program_id(1)
    start = pl.multiple_of(c * CHUNK, CHUNK)
    kblk = k_ref[pl.ds(start, BAND_W), :]
    vblk = v_ref[pl.ds(start, BAND_W), :]
    q = (q_ref[...] * (HDB ** -0.5)).astype(BF16)
    hmask = _head_masks(CHUNK, HB, HDB)
    col = lax.broadcasted_iota(jnp.int32, (CHUNK, BAND_W), 1)
    valid = (c + first_chunk) * CHUNK + col >= BAND_CHUNKS * CHUNK
    out = jnp.zeros((CHUNK, DB), F32)
    for h in range(HB):
        s = _dot_nt(jnp.where(hmask[h], q, jnp.zeros_like(q)), kblk) + bias_ref[h]
        s = jnp.where(valid, s, -jnp.inf)
        e = jnp.exp(s - jnp.max(s, axis=-1, keepdims=True))
        p = e / jnp.sum(e, axis=-1, keepdims=True)
        out = jnp.where(hmask[h], _dot(p.astype(BF16), vblk), out)
    o_ref[...] = out.astype(o_ref.dtype)


def _band_attention(z, q_col, k_bf16, v_bf16, bias, *, batch, n_chunks, first_chunk):
    tk = k_bf16.shape[1]
    assert tk == (n_chunks + BAND_CHUNKS) * CHUNK
    return pl.pallas_call(
        functools.partial(_band_kernel, first_chunk=first_chunk),
        grid=(batch, n_chunks),
        in_specs=[pl.BlockSpec((CHUNK, DB), lambda b, c: (b * n_chunks + c, q_col)),
                  pl.BlockSpec((None, tk, DB), lambda b, c: (b, 0, 0)),
                  pl.BlockSpec((None, tk, DB), lambda b, c: (b, 0, 0)),
                  pl.BlockSpec((HB, CHUNK, BAND_W), lambda b, c: (0, 0, 0))],
        out_specs=pl.BlockSpec((CHUNK, DB), lambda b, c: (b * n_chunks + c, 0)),
        out_shape=jax.ShapeDtypeStruct((batch * n_chunks * CHUNK, DB), BF16),
        compiler_params=_cparams("parallel", "arbitrary"),
        name="band_attention",
    )(z, k_bf16, v_bf16, bias)


def _mlstm_kernel(qc_ref, kc_ref, vc_ref, oc_ref, g_ref, convw_ref, convb_ref, bif_ref, ng_ref,
                  cn0_ref, m0_ref, conv0_ref, o_ref, cn_ref, m_ref, ubuf, *, length):
    L = length
    c = pl.program_id(1)

    @pl.when(c == 0)
    def _():
        cn_ref[...] = cn0_ref[...]
        m_ref[...] = m0_ref[...]
        ubuf[0:8, :] = conv0_ref[...]

    ubuf[8:8 + L, 0:DC] = qc_ref[...]
    ubuf[8:8 + L, DC:2 * DC] = kc_ref[...]
    y = convb_ref[...]
    for w in range(CONV_W):
        y = y + ubuf[8 - (CONV_W - 1) + w:8 - (CONV_W - 1) + w + L, :] * convw_ref[w:w + 1, :]
    tail = ubuf[L:L + 8, :]
    ubuf[0:8, :] = tail
    qk = y * jax.nn.sigmoid(y)

    g = g_ref[...] + bif_ref[...]
    lf_all, _ = _log_sigmoid_pair(g)
    r_i = lax.broadcasted_iota(jnp.int32, (L, L), 0)
    c_i = lax.broadcasted_iota(jnp.int32, (L, L), 1)
    causal = c_i <= r_i
    tril = jnp.where(causal, 1.0, 0.0).astype(BF16)
    hi, mid, lo = _split3(lf_all)
    b_all = _dot(tril, hi) + _dot(tril, mid) + _dot(tril, lo)
    lane = lax.broadcasted_iota(jnp.int32, (L, LANES), 1)
    mixed = jnp.where(lane < HC, g, b_all)
    eye = jnp.where(lax.broadcasted_iota(jnp.int32, (16, LANES), 0)
                    == lax.broadcasted_iota(jnp.int32, (16, LANES), 1), 1.0, 0.0).astype(BF16)
    hi, mid, lo = _split3(mixed)
    rows = _dot_nt(eye, hi) + _dot_nt(eye, mid) + _dot_nt(eye, lo)
    ones_col = jnp.where(lax.broadcasted_iota(jnp.int32, (L, HDC), 1) == 0, 1.0, 0.0).astype(BF16)

    for h in range(HC):
        li_row = rows[h:h + 1, :]
        b_row = rows[HC + h:HC + h + 1, :]
        li_col = g[:, h:h + 1]
        b_col = b_all[:, HC + h:HC + h + 1]
        dmat = jnp.where(causal, b_col - b_row + li_row, -jnp.inf)
        m_prev = m_ref[h][0:1, 0:1]
        inter = b_col + m_prev
        m_t = jnp.maximum(inter, jnp.max(dmat, axis=-1, keepdims=True))
        w_intra = jnp.exp(dmat - m_t)
        w_inter = jnp.exp(inter - m_t)
        qh = qk[:, h * HDC:(h + 1) * HDC].astype(BF16)
        kf = qk[:, DC + h * HDC:DC + (h + 1) * HDC] * (HDC ** -0.5)
        vaug = jnp.concatenate([vc_ref[:, h * HDC:(h + 1) * HDC].astype(BF16), ones_col], axis=1)
        s = _dot_nt(qh, kf.astype(BF16)) * w_intra
        cn = cn_ref[h]
        qcn = _dot(qh, cn.astype(BF16))
        sv = _dot(s.astype(BF16), vaug[:, :HDC])
        num = w_inter * qcn[:, :HDC] + sv
        den = w_inter * qcn[:, HDC:HDC + 1] + jnp.sum(s, axis=-1, keepdims=True)
        hh = num / jnp.maximum(jnp.abs(den), jnp.exp(-m_t))
        m_new = m_t[L - 1:L, :]
        decay = jnp.exp(inter[L - 1:L, :] - m_new)
        w_k = jnp.exp(b_col[L - 1:L, :] - b_col + li_col - m_new)
        cn_ref[h] = decay * cn + _dot_tn((kf * w_k).astype(BF16), vaug)
        m_ref[h] = jnp.broadcast_to(m_new, (8, LANES))
        mu = jnp.mean(hh, axis=-1, keepdims=True)
        dev = hh - mu
        var = jnp.mean(dev * dev, axis=-1, keepdims=True)
        hn = dev * lax.rsqrt(var + LN_EPS) * ng_ref[:, h * HDC:(h + 1) * HDC]
        gate = jax.nn.sigmoid(oc_ref[:, h * HDC:(h + 1) * HDC])
        o_ref[:, h * HDC:(h + 1) * HDC] = (gate * hn).astype(o_ref.dtype)


def _mlstm(z, conv_w, conv_b, bif_pad, norm_g, cn0, m0, conv0, *, batch, t, length):
    nc = t // length
    qcol, kcol, vcol, ocol = ((3 * DA + 3 * DB) // DC + j for j in range(4))
    zspec = lambda col: pl.BlockSpec((length, DC), lambda b, c: (b * nc + c, col))
    const = lambda shape: pl.BlockSpec(shape, lambda b, c: (0,) * len(shape))
    return pl.pallas_call(
        functools.partial(_mlstm_kernel, length=length),
        grid=(batch, nc),
        in_specs=[zspec(qcol), zspec(kcol), zspec(vcol), zspec(ocol),
                  pl.BlockSpec((length, LANES), lambda b, c: (b * nc + c, Z_MAIN // LANES)),
                  const((CONV_W, 2 * DC)), const((1, 2 * DC)), const((1, LANES)), const((1, DC)),
                  pl.BlockSpec((None, HC, HDC, 2 * HDC), lambda b, c: (b, 0, 0, 0)),
                  pl.BlockSpec((None, HC, 8, LANES), lambda b, c: (b, 0, 0, 0)),
                  pl.BlockSpec((None, 8, 2 * DC), lambda b, c: (b, 0, 0))],
        out_specs=[pl.BlockSpec((length, DC), lambda b, c: (b * nc + c, 0)),
                   pl.BlockSpec((None, HC, HDC, 2 * HDC), lambda b, c: (b, 0, 0, 0)),
                   pl.BlockSpec((None, HC, 8, LANES), lambda b, c: (b, 0, 0, 0))],
        out_shape=[jax.ShapeDtypeStruct((batch * t, DC), BF16),
                   jax.ShapeDtypeStruct((batch, HC, HDC, 2 * HDC), F32),
                   jax.ShapeDtypeStruct((batch, HC, 8, LANES), F32)],
        scratch_shapes=[pltpu.VMEM((8 + length, 2 * DC), F32)],
        compiler_params=_cparams("parallel", "arbitrary"),
        name="mlstm",
    )(z, z, z, z, z, conv_w, conv_b, bif_pad, norm_g, cn0, m0, conv0)


def _layer_norm(y, g, b):
    mu = jnp.mean(y, axis=-1, keepdims=True)
    dev = y - mu
    var = jnp.mean(dev * dev, axis=-1, keepdims=True)
    return dev * lax.rsqrt(var + LN_EPS) * g + b


def _out_proj_kernel(x_ref, a_ref, b_ref, c_ref, w_ref, g_ref, beta_ref, o_ref):
    mix = (_dot(a_ref[...], w_ref[0:DA, :]) + _dot(b_ref[...], w_ref[DA:DA + DB, :])
           + _dot(c_ref[...], w_ref[DA + DB:, :]))
    o_ref[...] = _layer_norm(ALPHA * x_ref[...] + mix, g_ref[...], beta_ref[...])


def _out_proj(x2d, a, b, c, w_bf16, g, beta, tm):
    n = x2d.shape[0]
    row = lambda w: pl.BlockSpec((tm, w), lambda i: (i, 0))
    const = lambda shape: pl.BlockSpec(shape, lambda i: (0, 0))
    return pl.pallas_call(
        _out_proj_kernel,
        grid=(n // tm,),
        in_specs=[row(D_MODEL), row(DA), row(DB), row(DC), const((D_MODEL, D_MODEL)),
                  const((1, D_MODEL)), const((1, D_MODEL))],
        out_specs=row(D_MODEL),
        out_shape=jax.ShapeDtypeStruct((n, D_MODEL), F32),
        compiler_params=_cparams("parallel"),
        name="out_proj_ln",
    )(x2d, a, b, c, w_bf16, g, beta)


def _route(logits):
    lane = lax.broadcasted_iota(jnp.int32, logits.shape, 1)
    big = jnp.int32(LANES)
    is_g = lane < N_GROUPS
    lg = jnp.where(is_g, logits, -jnp.inf)
    g_max = jnp.max(lg, axis=-1, keepdims=True)
    g_sel = jnp.min(jnp.where(lg == g_max, lane, big), axis=-1, keepdims=True)
    p_g = 1.0 / jnp.sum(jnp.where(is_g, jnp.exp(logits - g_max), 0.0), axis=-1, keepdims=True)
    e_lo = N_GROUPS + g_sel * E_PER_GROUP
    in_group = (lane >= e_lo) & (lane < e_lo + E_PER_GROUP)
    le = jnp.where(in_group, logits, -jnp.inf)
    v1 = jnp.max(le, axis=-1, keepdims=True)
    i1 = jnp.min(jnp.where(le == v1, lane, big), axis=-1, keepdims=True)
    le2 = jnp.where(lane == i1, -jnp.inf, le)
    v2 = jnp.max(le2, axis=-1, keepdims=True)
    i2 = jnp.min(jnp.where(le2 == v2, lane, big), axis=-1, keepdims=True)
    e2 = jnp.exp(v2 - v1)
    tot = 1.0 + e2
    return jnp.where(lane == i1, p_g * (1.0 / tot), 0.0) + jnp.where(lane == i2, p_g * (e2 / tot), 0.0)


def _moe_kernel(x_ref, rw_ref, rb_ref, wg_ref, wu_ref, wd_ref, g_ref, beta_ref, o_ref,
                xb_ref, comb_ref, acc_ref):
    e = pl.program_id(1)

    @pl.when(e == 0)
    def _():
        xb = x_ref[...].astype(BF16)
        xb_ref[...] = xb
        comb_ref[...] = _route(_dot(xb, rw_ref[...]) + rb_ref[...])
        acc_ref[...] = jnp.zeros_like(acc_ref)

    xb = xb_ref[...]
    gate = _dot(xb, wg_ref[...])
    h = gate * jax.nn.sigmoid(gate) * _dot(xb, wu_ref[...])
    y = _dot(h.astype(BF16), wd_ref[...])
    lane = lax.broadcasted_iota(jnp.int32, comb_ref.shape, 1)
    comb_e = jnp.sum(jnp.where(lane == N_GROUPS + e, comb_ref[...], 0.0), axis=-1, keepdims=True)
    acc_ref[...] += comb_e * y

    @pl.when(e == N_EXPERTS - 1)
    def _():
        o_ref[...] = _layer_norm(ALPHA * x_ref[...] + acc_ref[...], g_ref[...], beta_ref[...])


def _moe(x2d, rw_bf16, rb, wg, wu, wd, g, beta, tm):
    n = x2d.shape[0]
    const = lambda shape: pl.BlockSpec(shape, lambda i, e: (0, 0))
    return pl.pallas_call(
        _moe_kernel,
        grid=(n // tm, N_EXPERTS),
        in_specs=[pl.BlockSpec((tm, D_MODEL), lambda i, e: (i, 0)),
                  const((D_MODEL, LANES)), const((1, LANES)),
                  pl.BlockSpec((None, D_MODEL, D_EXPERT), lambda i, e: (e, 0, 0)),
                  pl.BlockSpec((None, D_MODEL, D_EXPERT), lambda i, e: (e, 0, 0)),
                  pl.BlockSpec((None, D_EXPERT, D_MODEL), lambda i, e: (e, 0, 0)),
                  const((1, D_MODEL)), const((1, D_MODEL))],
        out_specs=pl.BlockSpec((tm, D_MODEL), lambda i, e: (i, 0)),
        out_shape=jax.ShapeDtypeStruct((n, D_MODEL), F32),
        scratch_shapes=[pltpu.VMEM((tm, D_MODEL), BF16), pltpu.VMEM((tm, LANES), F32),
                        pltpu.VMEM((tm, D_MODEL), F32)],
        compiler_params=_cparams("parallel", "arbitrary"),
        name="moe_ln",
    )(x2d, rw_bf16, rb, wg, wu, wd, g, beta)


def _band_bias(rel_table):
    qi = jnp.arange(CHUNK)[:, None]
    kk = jnp.arange(BAND_W)[None, :] - BAND_CHUNKS * CHUNK
    return rel_table[:, jnp.clip(qi - kk, -REL_MAX, REL_MAX) + REL_MAX].astype(F32)


def _layer_weights(l, w_in, b_if, conv_w, conv_b, rel_table, c_norm_g, w_out, ln1_g, ln1_b,
                   router_g_w, router_g_b, router_e_w, router_e_b, exp_w_gate, exp_w_up, exp_w_down,
                   ln2_g, ln2_b):
    pad_cols = lambda a: jnp.pad(a, ((0, 0), (0, LANES - a.shape[1])))
    n_route = N_GROUPS + N_EXPERTS
    return dict(
        w_in=jnp.concatenate([w_in[l, :, :Z_MAIN], pad_cols(w_in[l, :, Z_MAIN:])], axis=1).astype(BF16),
        bif=pad_cols(b_if[l][None, :]),
        conv_w=conv_w[l], conv_b=conv_b[l][None, :],
        bias=_band_bias(rel_table[l]),
        norm_g=c_norm_g[l].reshape(1, DC),
        w_out=w_out[l].astype(BF16),
        ln1_g=ln1_g[l][None, :], ln1_b=ln1_b[l][None, :],
        rw=pad_cols(jnp.concatenate([router_g_w[l], router_e_w[l]], axis=1)).astype(BF16),
        rb=pad_cols(jnp.concatenate([router_g_b[l], router_e_b[l]])[None, :]),
        wg=exp_w_gate[l].astype(BF16), wu=exp_w_up[l].astype(BF16), wd=exp_w_down[l].astype(BF16),
        ln2_g=ln2_g[l][None, :], ln2_b=ln2_b[l][None, :],
    )


def _col(z, lo, width, batch, t, dtype=None):
    a = z[:, lo:lo + width]
    if dtype is not None:
        a = a.astype(dtype)
    return a.reshape(batch, t, width)


def _layer(x, w, cache, *, tm, tm_moe, sb_bq, length):
    batch, t, _ = x.shape
    n = batch * t
    tm, tm_moe = min(tm, n), min(tm_moe, n)
    x2d = x.reshape(n, D_MODEL)
    z = _in_proj(x2d, w["w_in"], tm)
    ka, va = _col(z, DA, DA, batch, t), _col(z, 2 * DA, DA, batch, t)
    kb, vb = _col(z, 3 * DA + DB, DB, batch, t), _col(z, 3 * DA + 2 * DB, DB, batch, t)
    conv_tail = _col(z, 3 * DA + 3 * DB, 2 * DC, batch, t)[:, t - (CONV_W - 1):]

    if cache is None:
        past = 0
        sb_k, sb_v = ka.astype(BF16), va.astype(BF16)
        band_pad = ((0, 0), (BAND_CHUNKS * CHUNK, 0), (0, 0))
        band_k, band_v = jnp.pad(kb.astype(BF16), band_pad), jnp.pad(vb.astype(BF16), band_pad)
        first_chunk = 0
        cn0 = jnp.zeros((batch, HC, HDC, 2 * HDC), F32)
        m0 = jnp.zeros((batch, HC, 8, LANES), F32)
        conv0 = jnp.zeros((batch, 8, 2 * DC), F32)
    else:
        a_k, a_v, b_k, b_v, c0, n0, m_init, conv_init = cache
        past = a_k.shape[1]
        tk_pad = -(past + t) % SB_BLOCK
        sb_cat = lambda old, new: jnp.pad(
            jnp.concatenate([old.reshape(batch, past, DA), new], axis=1).astype(BF16),
            ((0, 0), (0, tk_pad), (0, 0)))
        sb_k, sb_v = sb_cat(a_k, ka), sb_cat(a_v, va)
        hist = b_k.shape[1]
        assert hist == BAND_CHUNKS * CHUNK and t == CHUNK and past % CHUNK == 0
        band_cat = lambda old, new: jnp.concatenate([old.reshape(batch, hist, DB), new], axis=1).astype(BF16)
        band_k, band_v = band_cat(b_k, kb), band_cat(b_v, vb)
        first_chunk = BAND_CHUNKS
        cn0 = jnp.concatenate([c0, n0[..., None], jnp.zeros((batch, HC, HDC, HDC - 1), F32)], axis=-1)
        m0 = jnp.broadcast_to(m_init[:, :, None, None], (batch, HC, 8, LANES))
        conv0 = jnp.pad(conv_init, ((0, 0), (8 - (CONV_W - 1), 0), (0, 0)))

    a_out = _sb_attention(z, 0, sb_k, sb_v, batch=batch, tq=t, past=past, bq=sb_bq)
    b_out = _band_attention(z, (3 * DA) // DB, band_k, band_v, w["bias"], batch=batch,
                            n_chunks=t // CHUNK, first_chunk=first_chunk)
    c_out, cn1, m1 = _mlstm(z, w["conv_w"], w["conv_b"], w["bif"], w["norm_g"], cn0, m0, conv0,
                            batch=batch, t=t, length=length)
    x1 = _out_proj(x2d, a_out, b_out, c_out, w["w_out"], w["ln1_g"], w["ln1_b"], tm)
    x2 = _moe(x1, w["rw"], w["rb"], w["wg"], w["wu"], w["wd"], w["ln2_g"], w["ln2_b"], tm_moe)
    state = (ka.reshape(batch, t, HA, HDA), va.reshape(batch, t, HA, HDA),
             kb.reshape(batch, t, HB, HDB), vb.reshape(batch, t, HB, HDB),
             cn1[..., :HDC], cn1[..., HDC], m1[:, :, 0, 0], conv_tail)
    return x2.reshape(batch, t, D_MODEL), state


def kernel(x_prompt, x_sample, cache_a_k, cache_a_v, cache_b_k, cache_b_v, state_c_C, state_c_n,
           state_c_m, state_c_conv, w_in, b_if, conv_w, conv_b, rel_table, c_norm_g, w_out, ln1_g, ln1_b,
           router_g_w, router_g_b, router_e_w, router_e_b, exp_w_gate, exp_w_up, exp_w_down, ln2_g, ln2_b):
    band_keep = cache_b_k.shape[2]
    xp, xs = x_prompt, x_sample
    prompt_states, sample_states = [], []
    for l in range(DEPTH):
        w = _layer_weights(l, w_in, b_if, conv_w, conv_b, rel_table, c_norm_g, w_out, ln1_g, ln1_b,
                           router_g_w, router_g_b, router_e_w, router_e_b, exp_w_gate, exp_w_up,
                           exp_w_down, ln2_g, ln2_b)
        xp, sp = _layer(xp, w, None, tm=512, tm_moe=1024, sb_bq=SB_BLOCK, length=CHUNK)
        ak, av, bk, bv, cc, cn, cm, cconv = sp
        prompt_states.append((ak, av, bk[:, -band_keep:], bv[:, -band_keep:], cc, cn, cm, cconv))
        cache_l = (cache_a_k[l], cache_a_v[l], cache_b_k[l], cache_b_v[l],
                   state_c_C[l], state_c_n[l], state_c_m[l], state_c_conv[l])
        xs, ss = _layer(xs, w, cache_l, tm=512, tm_moe=512, sb_bq=CHUNK, length=CHUNK)
        sample_states.append(ss)
    p = [jnp.stack(s) for s in zip(*prompt_states)]
    s = [jnp.stack(s) for s in zip(*sample_states)]
    return (xp, xs, *p, *s)
```

```python
import functools

import jax
import jax.numpy as jnp
from jax import lax
from jax.experimental import pallas as pl
from jax.experimental.pallas import tpu as pltpu

F32 = jnp.float32
BF16 = jnp.bfloat16

D_MODEL = 1024
DEPTH = 2
CHUNK = 64
HA, HDA = 4, 64
HB, HDB = 4, 64
HC, HDC = 4, 128
DA, DB, DC = HA * HDA, HB * HDB, HC * HDC
BAND_CHUNKS = 8
BAND_W = (BAND_CHUNKS + 1) * CHUNK
REL_MAX = 128
CONV_W = 4
N_GROUPS = 4
E_PER_GROUP = 4
N_EXPERTS = N_GROUPS * E_PER_GROUP
D_EXPERT = 512
ALPHA = (2 * DEPTH) ** 0.25
LN_EPS = 1e-5

LANES = 128
Z_MAIN = 3 * DA + 3 * DB + 4 * DC
Z_COLS = Z_MAIN + LANES
SB_BLOCK = 128
SB_LOG_CUTOFF = -104.0
BAND_GROUP = 4
VMEM_LIMIT = 48 * 1024 * 1024


def _cparams(*sem):
    return pltpu.CompilerParams(dimension_semantics=sem, vmem_limit_bytes=VMEM_LIMIT)


def _split3(x):
    hi = x.astype(BF16)
    r1 = x - hi.astype(F32)
    mid = r1.astype(BF16)
    lo = (r1 - mid.astype(F32)).astype(BF16)
    return hi, mid, lo


def _dot(a, b):
    return jnp.dot(a, b, preferred_element_type=F32)


def _dot_nt(a, b):
    return lax.dot_general(a, b, (((1,), (1,)), ((), ())), preferred_element_type=F32)


def _dot_tn(a, b):
    return lax.dot_general(a, b, (((0,), (0,)), ((), ())), preferred_element_type=F32)


def _log_sigmoid_pair(z):
    t = jnp.log1p(jnp.exp(-jnp.abs(z)))
    return -(jnp.maximum(-z, 0.0) + t), -(jnp.maximum(z, 0.0) + t)


def _stack_heads(q, n_heads, width):
    lane = lax.broadcasted_iota(jnp.int32, q.shape, 1)
    zero = jnp.zeros_like(q)
    return jnp.concatenate(
        [jnp.where((lane >= h * width) & (lane < (h + 1) * width), q, zero) for h in range(n_heads)], axis=0)


def _unstack_heads(acc, n_heads, width):
    rows = acc.shape[0] // n_heads
    lane = lax.broadcasted_iota(jnp.int32, (rows, n_heads * width), 1)
    out = acc[0:rows]
    for h in range(1, n_heads):
        out = jnp.where(lane >= h * width, acc[h * rows:(h + 1) * rows], out)
    return out


def _in_proj_kernel(x_ref, w_ref, qa, ka, va, kab, vab, qb, kb, vb, kbb, vbb, u, vcb, oc, g):
    xb = x_ref[...].astype(BF16)

    def proj(lo, width):
        return _dot(xb, w_ref[:, lo:lo + width])

    qa[...] = (proj(0, DA) * (HDA ** -0.5)).astype(BF16)
    for f32_ref, bf_ref, lo in ((ka, kab, DA), (va, vab, 2 * DA), (kb, kbb, 3 * DA + DB), (vb, vbb, 3 * DA + 2 * DB)):
        t = proj(lo, DA)
        f32_ref[...] = t
        bf_ref[...] = t.astype(BF16)
    qb[...] = (proj(3 * DA, DB) * (HDB ** -0.5)).astype(BF16)
    base = 3 * DA + 3 * DB
    for c in range(2):
        u[:, c * DC:(c + 1) * DC] = proj(base + c * DC, DC)
    vcb[...] = proj(base + 2 * DC, DC).astype(BF16)
    oc[...] = proj(base + 3 * DC, DC)
    g[...] = proj(Z_MAIN, LANES)


def _in_proj(x2d, w_bf16, tm):
    n = x2d.shape[0]
    widths_dtypes = [(DA, BF16), (DA, F32), (DA, F32), (DA, BF16), (DA, BF16),
                     (DB, BF16), (DB, F32), (DB, F32), (DB, BF16), (DB, BF16),
                     (2 * DC, F32), (DC, BF16), (DC, F32), (LANES, F32)]
    return pl.pallas_call(
        _in_proj_kernel,
        grid=(n // tm,),
        in_specs=[pl.BlockSpec((tm, D_MODEL), lambda i: (i, 0)),
                  pl.BlockSpec((D_MODEL, Z_COLS), lambda i: (0, 0))],
        out_specs=[pl.BlockSpec((tm, w), lambda i: (i, 0)) for w, _ in widths_dtypes],
        out_shape=[jax.ShapeDtypeStruct((n, w), dt) for w, dt in widths_dtypes],
        compiler_params=_cparams("parallel"),
        name="in_proj",
    )(x2d, w_bf16)


def _sb_kernel(q_ref, k_ref, v_ref, o_ref, carry_ref, acc_ref, *, past, bq, bk):
    i = pl.program_id(1)
    qs = _stack_heads(q_ref[...], HA, HDA)
    jj = lax.broadcasted_iota(jnp.int32, (bk, 2 * bk), 0)
    ss = lax.broadcasted_iota(jnp.int32, (bk, 2 * bk), 1)
    csum = jnp.where((ss >= bk) | (jj > ss), 1.0, 0.0).astype(BF16)
    kb_last = (past + (i + 1) * bq - 2) // bk

    def block(kb, diagonal):
        start = pl.multiple_of(kb * bk, bk)
        s = _dot_nt(qs, k_ref[pl.ds(start, bk), :])
        ls_pos, ls_neg = _log_sigmoid_pair(s)
        if diagonal:
            row = lax.broadcasted_iota(jnp.int32, (HA * bq, bk), 0) & (bq - 1)
            mask = (start + lax.broadcasted_iota(jnp.int32, (HA * bq, bk), 1)) < past + i * bq + row
            log_stay = jnp.where(mask, ls_neg, 0.0)
        else:
            log_stay = ls_neg
        hi = log_stay.astype(BF16)
        lo = (log_stay - hi.astype(F32)).astype(BF16)
        cs = _dot(hi, csum) + _dot(lo, csum)
        if diagonal:
            w = jnp.where(mask, jnp.exp(ls_pos + cs[:, :bk]), 0.0)
            carry = cs[:, bk:]
            acc_ref[...] = _dot(w.astype(BF16), v_ref[pl.ds(start, bk), :])
        else:
            carry = carry_ref[...]
            w = jnp.exp(ls_pos + carry + cs[:, :bk])
            carry = carry + cs[:, bk:]
            acc_ref[...] += _dot(w.astype(BF16), v_ref[pl.ds(start, bk), :])
        carry_ref[...] = carry
        return (jnp.max(carry) < SB_LOG_CUTOFF).astype(jnp.int32)

    done0 = block(kb_last, True)

    def cond(state):
        kb, done = state
        return jnp.logical_and(kb >= 0, done == 0)

    def body(state):
        kb, _ = state
        return kb - 1, block(kb, False)

    lax.while_loop(cond, body, (kb_last - 1, done0))
    o_ref[...] = _unstack_heads(acc_ref[...], HA, HDA).astype(o_ref.dtype)


def _sb_attention(q_bf16, k_bf16, v_bf16, *, batch, tq, past, bq):
    bk = SB_BLOCK
    nq = tq // bq
    tk = k_bf16.shape[1]
    assert tk % bk == 0 and (past + tq - 2) // bk < tk // bk
    assert past % bk == 0 and (bq == bk or (nq == 1 and bq <= bk))
    return pl.pallas_call(
        functools.partial(_sb_kernel, past=past, bq=bq, bk=bk),
        grid=(batch, nq),
        in_specs=[pl.BlockSpec((bq, DA), lambda b, i: (b * nq + i, 0)),
                  pl.BlockSpec((None, tk, DA), lambda b, i: (b, 0, 0)),
                  pl.BlockSpec((None, tk, DA), lambda b, i: (b, 0, 0))],
        out_specs=pl.BlockSpec((bq, DA), lambda b, i: (b * nq + i, 0)),
        out_shape=jax.ShapeDtypeStruct((batch * tq, DA), BF16),
        scratch_shapes=[pltpu.VMEM((HA * bq, bk), F32), pltpu.VMEM((HA * bq, DA), F32)],
        compiler_params=_cparams("parallel", "arbitrary"),
        name="sb_attention",
    )(q_bf16, k_bf16, v_bf16)


def _band_kernel(q_ref, *refs, n_kb, rows_per_pass):
    k_refs, v_refs = refs[:n_kb], refs[n_kb:2 * n_kb]
    bias_ref, o_ref, s_ref, p_ref = refs[2 * n_kb:]
    j = pl.program_id(1)
    qs = _stack_heads(q_ref[...], HB, HDB)
    kbs = k_refs[0].shape[0]
    for i in range(n_kb):
        s = _dot_nt(qs, k_refs[i][...]) + bias_ref[:, i * kbs:(i + 1) * kbs]
        if i < n_kb - 1:
            s = s + jnp.where(j + i < n_kb - 1, -jnp.inf, 0.0)
        s_ref[:, i * kbs:(i + 1) * kbs] = s
    for r in range(0, s_ref.shape[0], rows_per_pass):
        s = s_ref[r:r + rows_per_pass, :]
        e = jnp.exp(s - jnp.max(s, axis=-1, keepdims=True))
        p_ref[r:r + rows_per_pass, :] = (e * (1.0 / jnp.sum(e, axis=-1, keepdims=True))).astype(BF16)
    acc = _dot(p_ref[:, 0:kbs], v_refs[0][...])
    for i in range(1, n_kb):
        acc = acc + _dot(p_ref[:, i * kbs:(i + 1) * kbs], v_refs[i][...])
    o_ref[...] = _unstack_heads(acc, HB, HDB).astype(o_ref.dtype)


def _band_attention(q_bf16, k_bf16, v_bf16, bias, *, batch, steps, gq, n_kb, kbs):
    blocks_per_batch = k_bf16.shape[0] // (batch * kbs)
    kv_specs = [pl.BlockSpec((kbs, DB), functools.partial(
        lambda b, j, i: (b * blocks_per_batch + jnp.maximum(j + i - (n_kb - 1), 0), 0), i=i)) for i in range(n_kb)]
    return pl.pallas_call(
        functools.partial(_band_kernel, n_kb=n_kb, rows_per_pass=min(128, HB * gq)),
        grid=(batch, steps),
        in_specs=[pl.BlockSpec((gq, DB), lambda b, j: (b * steps + j, 0))] + kv_specs + kv_specs
        + [pl.BlockSpec((HB * gq, n_kb * kbs), lambda b, j: (0, 0))],
        out_specs=pl.BlockSpec((gq, DB), lambda b, j: (b * steps + j, 0)),
        out_shape=jax.ShapeDtypeStruct((batch * steps * gq, DB), BF16),
        scratch_shapes=[pltpu.VMEM((HB * gq, n_kb * kbs), F32), pltpu.VMEM((HB * gq, n_kb * kbs), BF16)],
        compiler_params=_cparams("parallel", "arbitrary"),
        name="band_attention",
    )(q_bf16, *([k_bf16] * n_kb), *([v_bf16] * n_kb), bias)


def _mlstm_kernel(u_ref, vc_ref, oc_ref, g_ref, convw_ref, convb_ref, bif_ref, ng_ref,
                  cn0_ref, m0_ref, conv0_ref, o_ref, cn_ref, m_ref, ubuf, *, length):
    L = length
    c = pl.program_id(1)

    @pl.when(c == 0)
    def _():
        cn_ref[...] = cn0_ref[...]
        m_ref[...] = m0_ref[...]
        ubuf[0:8, :] = conv0_ref[...]

    ubuf[8:8 + L, :] = u_ref[...]
    y = convb_ref[...]
    for w in range(CONV_W):
        y = y + ubuf[8 - (CONV_W - 1) + w:8 - (CONV_W - 1) + w + L, :] * convw_ref[w:w + 1, :]
    tail = ubuf[L:L + 8, :]
    ubuf[0:8, :] = tail
    qk = y * jax.nn.sigmoid(y)

    g = g_ref[...] + bif_ref[...]
    lf_all, _ = _log_sigmoid_pair(g)
    r_i = lax.broadcasted_iota(jnp.int32, (L, L), 0)
    c_i = lax.broadcasted_iota(jnp.int32, (L, L), 1)
    causal = c_i <= r_i
    tril = jnp.where(causal, 1.0, 0.0).astype(BF16)
    hi, mid, lo = _split3(lf_all)
    b_all = _dot(tril, hi) + _dot(tril, mid) + _dot(tril, lo)
    lane = lax.broadcasted_iota(jnp.int32, (L, LANES), 1)
    mixed = jnp.where(lane < HC, g, b_all)
    eye = jnp.where(lax.broadcasted_iota(jnp.int32, (16, LANES), 0)
                    == lax.broadcasted_iota(jnp.int32, (16, LANES), 1), 1.0, 0.0).astype(BF16)
    hi, mid, lo = _split3(mixed)
    rows = _dot_nt(eye, hi) + _dot_nt(eye, mid) + _dot_nt(eye, lo)
    ones_col = jnp.where(lax.broadcasted_iota(jnp.int32, (L, HDC), 1) == 0, 1.0, 0.0).astype(BF16)

    for h in range(HC):
        li_row = rows[h:h + 1, :]
        b_row = rows[HC + h:HC + h + 1, :]
        li_col = g[:, h:h + 1]
        b_col = b_all[:, HC + h:HC + h + 1]
        dmat = jnp.where(causal, b_col - b_row + li_row, -jnp.inf)
        m_prev = m_ref[h][0:1, 0:1]
        inter = b_col + m_prev
        m_t = jnp.maximum(inter, jnp.max(dmat, axis=-1, keepdims=True))
        w_intra = jnp.exp(dmat - m_t)
        w_inter = jnp.exp(inter - m_t)
        qh = qk[:, h * HDC:(h + 1) * HDC].astype(BF16)
        kf = qk[:, DC + h * HDC:DC + (h + 1) * HDC] * (HDC ** -0.5)
        vaug = jnp.concatenate([vc_ref[:, h * HDC:(h + 1) * HDC], ones_col], axis=1)
        s = _dot_nt(qh, kf.astype(BF16)) * w_intra
        cn = cn_ref[h]
        qcn = _dot(qh, cn.astype(BF16))
        sv = _dot(s.astype(BF16), vaug[:, :HDC])
        num = w_inter * qcn[:, :HDC] + sv
        den = w_inter * qcn[:, HDC:HDC + 1] + jnp.sum(s, axis=-1, keepdims=True)
        hh = num / jnp.maximum(jnp.abs(den), jnp.exp(-m_t))
        m_new = m_t[L - 1:L, :]
        decay = jnp.exp(inter[L - 1:L, :] - m_new)
        w_k = jnp.exp(b_col[L - 1:L, :] - b_col + li_col - m_new)
        cn_ref[h] = decay * cn + _dot_tn((kf * w_k).astype(BF16), vaug)
        m_ref[h] = jnp.broadcast_to(m_new, (8, LANES))
        mu = jnp.mean(hh, axis=-1, keepdims=True)
        dev = hh - mu
        var = jnp.mean(dev * dev, axis=-1, keepdims=True)
        hn = dev * lax.rsqrt(var + LN_EPS) * ng_ref[:, h * HDC:(h + 1) * HDC]
        gate = jax.nn.sigmoid(oc_ref[:, h * HDC:(h + 1) * HDC])
        o_ref[:, h * HDC:(h + 1) * HDC] = (gate * hn).astype(o_ref.dtype)


def _mlstm(u, vc_bf16, oc, gates, conv_w, conv_b, bif_pad, norm_g, cn0, m0, conv0, *, batch, t, length):
    nc = t // length
    row = lambda w: pl.BlockSpec((length, w), lambda b, c: (b * nc + c, 0))
    const = lambda shape: pl.BlockSpec(shape, lambda b, c: (0,) * len(shape))
    per_batch = lambda shape: pl.BlockSpec((None,) + shape, lambda b, c: (b,) + (0,) * len(shape))
    return pl.pallas_call(
        functools.partial(_mlstm_kernel, length=length),
        grid=(batch, nc),
        in_specs=[row(2 * DC), row(DC), row(DC), row(LANES),
                  const((CONV_W, 2 * DC)), const((1, 2 * DC)), const((1, LANES)), const((1, DC)),
                  per_batch((HC, HDC, 2 * HDC)), per_batch((HC, 8, LANES)), per_batch((8, 2 * DC))],
        out_specs=[row(DC), per_batch((HC, HDC, 2 * HDC)), per_batch((HC, 8, LANES))],
        out_shape=[jax.ShapeDtypeStruct((batch * t, DC), BF16),
                   jax.ShapeDtypeStruct((batch, HC, HDC, 2 * HDC), F32),
                   jax.ShapeDtypeStruct((batch, HC, 8, LANES), F32)],
        scratch_shapes=[pltpu.VMEM((8 + length, 2 * DC), F32)],
        compiler_params=_cparams("parallel", "arbitrary"),
        name="mlstm",
    )(u, vc_bf16, oc, gates, conv_w, conv_b, bif_pad, norm_g, cn0, m0, conv0)


def _layer_norm(y, g, b):
    mu = jnp.mean(y, axis=-1, keepdims=True)
    dev = y - mu
    var = jnp.mean(dev * dev, axis=-1, keepdims=True)
    return dev * lax.rsqrt(var + LN_EPS) * g + b


def _out_proj_kernel(x_ref, a_ref, b_ref, c_ref, w_ref, g_ref, beta_ref, o_ref):
    mix = (_dot(a_ref[...], w_ref[0:DA, :]) + _dot(b_ref[...], w_ref[DA:DA + DB, :])
           + _dot(c_ref[...], w_ref[DA + DB:, :]))
    o_ref[...] = _layer_norm(ALPHA * x_ref[...] + mix, g_ref[...], beta_ref[...])


def _out_proj(x2d, a, b, c, w_bf16, g, beta, tm):
    n = x2d.shape[0]
    row = lambda w: pl.BlockSpec((tm, w), lambda i: (i, 0))
    const = lambda shape: pl.BlockSpec(shape, lambda i: (0, 0))
    return pl.pallas_call(
        _out_proj_kernel,
        grid=(n // tm,),
        in_specs=[row(D_MODEL), row(DA), row(DB), row(DC), const((D_MODEL, D_MODEL)),
                  const((1, D_MODEL)), const((1, D_MODEL))],
        out_specs=row(D_MODEL),
        out_shape=jax.ShapeDtypeStruct((n, D_MODEL), F32),
        compiler_params=_cparams("parallel"),
        name="out_proj_ln",
    )(x2d, a, b, c, w_bf16, g, beta)


def _route(logits):
    lane = lax.broadcasted_iota(jnp.int32, logits.shape, 1)
    big = jnp.int32(LANES)
    is_g = lane < N_GROUPS
    lg = jnp.where(is_g, logits, -jnp.inf)
    g_max = jnp.max(lg, axis=-1, keepdims=True)
    g_sel = jnp.min(jnp.where(lg == g_max, lane, big), axis=-1, keepdims=True)
    p_g = 1.0 / jnp.sum(jnp.where(is_g, jnp.exp(logits - g_max), 0.0), axis=-1, keepdims=True)
    e_lo = N_GROUPS + g_sel * E_PER_GROUP
    in_group = (lane >= e_lo) & (lane < e_lo + E_PER_GROUP)
    le = jnp.where(in_group, logits, -jnp.inf)
    v1 = jnp.max(le, axis=-1, keepdims=True)
    i1 = jnp.min(jnp.where(le == v1, lane, big), axis=-1, keepdims=True)
    le2 = jnp.where(lane == i1, -jnp.inf, le)
    v2 = jnp.max(le2, axis=-1, keepdims=True)
    i2 = jnp.min(jnp.where(le2 == v2, lane, big), axis=-1, keepdims=True)
    e2 = jnp.exp(v2 - v1)
    tot = 1.0 + e2
    return jnp.where(lane == i1, p_g * (1.0 / tot), 0.0) + jnp.where(lane == i2, p_g * (e2 / tot), 0.0)


def _moe_kernel(x_ref, rw_ref, rb_ref, wg_ref, wu_ref, wd_ref, g_ref, beta_ref, o_ref,
                xb_ref, comb_ref, acc_ref):
    e = pl.program_id(1)

    @pl.when(e == 0)
    def _():
        xb = x_ref[...].astype(BF16)
        xb_ref[...] = xb
        comb_ref[...] = _route(_dot(xb, rw_ref[...]) + rb_ref[...])
        acc_ref[...] = jnp.zeros_like(acc_ref)

    xb = xb_ref[...]
    gate = _dot(xb, wg_ref[...])
    h = gate * jax.nn.sigmoid(gate) * _dot(xb, wu_ref[...])
    y = _dot(h.astype(BF16), wd_ref[...])
    lane = lax.broadcasted_iota(jnp.int32, comb_ref.shape, 1)
    comb_e = jnp.sum(jnp.where(lane == N_GROUPS + e, comb_ref[...], 0.0), axis=-1, keepdims=True)
    acc_ref[...] += comb_e * y

    @pl.when(e == N_EXPERTS - 1)
    def _():
        o_ref[...] = _layer_norm(ALPHA * x_ref[...] + acc_ref[...], g_ref[...], beta_ref[...])


def _moe(x2d, rw_bf16, rb, wg, wu, wd, g, beta, tm):
    n = x2d.shape[0]
    const = lambda shape: pl.BlockSpec(shape, lambda i, e: (0, 0))
    return pl.pallas_call(
        _moe_kernel,
        grid=(n // tm, N_EXPERTS),
        in_specs=[pl.BlockSpec((tm, D_MODEL), lambda i, e: (i, 0)),
                  const((D_MODEL, LANES)), const((1, LANES)),
                  pl.BlockSpec((None, D_MODEL, D_EXPERT), lambda i, e: (e, 0, 0)),
                  pl.BlockSpec((None, D_MODEL, D_EXPERT), lambda i, e: (e, 0, 0)),
                  pl.BlockSpec((None, D_EXPERT, D_MODEL), lambda i, e: (e, 0, 0)),
                  const((1, D_MODEL)), const((1, D_MODEL))],
        out_specs=pl.BlockSpec((tm, D_MODEL), lambda i, e: (i, 0)),
        out_shape=jax.ShapeDtypeStruct((n, D_MODEL), F32),
        scratch_shapes=[pltpu.VMEM((tm, D_MODEL), BF16), pltpu.VMEM((tm, LANES), F32),
                        pltpu.VMEM((tm, D_MODEL), F32)],
        compiler_params=_cparams("parallel", "arbitrary"),
        name="moe_ln",
    )(x2d, rw_bf16, rb, wg, wu, wd, g, beta)


def _band_bias(rel_table):
    n_clipped = BAND_W - 1 - REL_MAX
    ext = jnp.concatenate([rel_table[:, REL_MAX - (CHUNK - 1):],
                           jnp.broadcast_to(rel_table[:, -1:], (HB, n_clipped))], axis=1)
    rev = ext[:, ::-1]
    return jnp.stack([rev[:, CHUNK - 1 - i:CHUNK - 1 - i + BAND_W] for i in range(CHUNK)], axis=1).astype(F32)


def _band_bias_grouped(bias):
    g_n = BAND_GROUP
    tabs = [jnp.pad(bias, ((0, 0), (0, 0), (g * CHUNK, (g_n - 1 - g) * CHUNK)), constant_values=-jnp.inf)
            for g in range(g_n)]
    return jnp.stack(tabs, axis=1).reshape(HB * g_n * CHUNK, (g_n + BAND_CHUNKS) * CHUNK)


def _layer_weights(l, w_in, b_if, conv_w, conv_b, rel_table, c_norm_g, w_out, ln1_g, ln1_b,
                   router_g_w, router_g_b, router_e_w, router_e_b, exp_w_gate, exp_w_up, exp_w_down,
                   ln2_g, ln2_b):
    pad_cols = lambda a: jnp.pad(a, ((0, 0), (0, LANES - a.shape[1])))
    bias = _band_bias(rel_table[l])
    return dict(
        w_in=jnp.concatenate([w_in[l, :, :Z_MAIN], pad_cols(w_in[l, :, Z_MAIN:])], axis=1).astype(BF16),
        bif=pad_cols(b_if[l][None, :]),
        conv_w=conv_w[l], conv_b=conv_b[l][None, :],
        bias=bias.reshape(HB * CHUNK, BAND_W), bias_grouped=_band_bias_grouped(bias),
        norm_g=c_norm_g[l].reshape(1, DC),
        w_out=w_out[l].astype(BF16),
        ln1_g=ln1_g[l][None, :], ln1_b=ln1_b[l][None, :],
        rw=pad_cols(jnp.concatenate([router_g_w[l], router_e_w[l]], axis=1)).astype(BF16),
        rb=pad_cols(jnp.concatenate([router_g_b[l], router_e_b[l]])[None, :]),
        wg=exp_w_gate[l].astype(BF16), wu=exp_w_up[l].astype(BF16), wd=exp_w_down[l].astype(BF16),
        ln2_g=ln2_g[l][None, :], ln2_b=ln2_b[l][None, :],
    )


def _layer(x, w, cache, *, tm, tm_moe, sb_bq, length):
    batch, t, _ = x.shape
    n = batch * t
    tm, tm_moe = min(tm, n), min(tm_moe, n)
    x2d = x.reshape(n, D_MODEL)
    qa, ka, va, ka_bf, va_bf, qb, kb, vb, kb_bf, vb_bf, u, vc_bf, oc, gates = _in_proj(x2d, w["w_in"], tm)
    per_batch = lambda a: a.reshape(batch, t, a.shape[-1])

    if cache is None:
        past = 0
        sb_k, sb_v = per_batch(ka_bf), per_batch(va_bf)
        gq = BAND_GROUP * CHUNK
        assert t % gq == 0 and BAND_CHUNKS * CHUNK % gq == 0
        band = dict(k=kb_bf, v=vb_bf, bias=w["bias_grouped"], steps=t // gq, gq=gq,
                    n_kb=BAND_CHUNKS * CHUNK // gq + 1, kbs=gq)
        cn0 = jnp.zeros((batch, HC, HDC, 2 * HDC), F32)
        m0 = jnp.zeros((batch, HC, 8, LANES), F32)
        conv0 = jnp.zeros((batch, 8, 2 * DC), F32)
    else:
        a_k, a_v, b_k, b_v, c0, n0, m_init, conv_init = cache
        past = a_k.shape[1]
        tk_pad = -(past + t) % SB_BLOCK
        sb_cat = lambda old, new: jnp.pad(
            jnp.concatenate([old.reshape(batch, past, DA).astype(BF16), per_batch(new)], axis=1),
            ((0, 0), (0, tk_pad), (0, 0)))
        sb_k, sb_v = sb_cat(a_k, ka_bf), sb_cat(a_v, va_bf)
        hist = b_k.shape[1]
        assert hist == BAND_CHUNKS * CHUNK and t == CHUNK and past % CHUNK == 0 and past >= hist
        band_cat = lambda old, new: jnp.concatenate(
            [old.reshape(batch, hist, DB).astype(BF16), per_batch(new)], axis=1).reshape(batch * BAND_W, DB)
        band = dict(k=band_cat(b_k, kb_bf), v=band_cat(b_v, vb_bf), bias=w["bias"], steps=1, gq=CHUNK,
                    n_kb=1, kbs=BAND_W)
        cn0 = jnp.concatenate([c0, n0[..., None], jnp.zeros((batch, HC, HDC, HDC - 1), F32)], axis=-1)
        m0 = jnp.broadcast_to(m_init[:, :, None, None], (batch, HC, 8, LANES))
        conv0 = jnp.pad(conv_init, ((0, 0), (8 - (CONV_W - 1), 0), (0, 0)))

    a_out = _sb_attention(qa, sb_k, sb_v, batch=batch, tq=t, past=past, bq=sb_bq)
    b_out = _band_attention(qb, band["k"], band["v"], band["bias"], batch=batch, steps=band["steps"],
                            gq=band["gq"], n_kb=band["n_kb"], kbs=band["kbs"])
    c_out, cn1, m1 = _mlstm(u, vc_bf, oc, gates, w["conv_w"], w["conv_b"], w["bif"], w["norm_g"],
                            cn0, m0, conv0, batch=batch, t=t, length=length)
    x1 = _out_proj(x2d, a_out, b_out, c_out, w["w_out"], w["ln1_g"], w["ln1_b"], tm)
    x2 = _moe(x1, w["rw"], w["rb"], w["wg"], w["wu"], w["wd"], w["ln2_g"], w["ln2_b"], tm_moe)
    state = (ka.reshape(batch, t, HA, HDA), va.reshape(batch, t, HA, HDA),
             kb.reshape(batch, t, HB, HDB), vb.reshape(batch, t, HB, HDB),
             cn1[..., :HDC], cn1[..., HDC], m1[:, :, 0, 0], per_batch(u)[:, t - (CONV_W - 1):])
    return x2.reshape(batch, t, D_MODEL), state


def kernel(x_prompt, x_sample, cache_a_k, cache_a_v, cache_b_k, cache_b_v, state_c_C, state_c_n,
           state_c_m, state_c_conv, w_in, b_if, conv_w, conv_b, rel_table, c_norm_g, w_out, ln1_g, ln1_b,
           router_g_w, router_g_b, router_e_w, router_e_b, exp_w_gate, exp_w_up, exp_w_down, ln2_g, ln2_b):
    band_keep = cache_b_k.shape[2]
    xp, xs = x_prompt, x_sample
    prompt_states, sample_states = [], []
    for l in range(DEPTH):
        w = _layer_weights(l, w_in, b_if, conv_w, conv_b, rel_table, c_norm_g, w_out, ln1_g, ln1_b,
                           router_g_w, router_g_b, router_e_w, router_e_b, exp_w_gate, exp_w_up,
                           exp_w_down, ln2_g, ln2_b)
        xp, sp = _layer(xp, w, None, tm=512, tm_moe=1024, sb_bq=SB_BLOCK, length=CHUNK)
        ak, av, bk, bv, cc, cn, cm, cconv = sp
        prompt_states.append((ak, av, bk[:, -band_keep:], bv[:, -band_keep:], cc, cn, cm, cconv))
        cache_l = (cache_a_k[l], cache_a_v[l], cache_b_k[l], cache_b_v[l],
                   state_c_C[l], state_c_n[l], state_c_m[l], state_c_conv[l])
        xs, ss = _layer(xs, w, cache_l, tm=512, tm_moe=512, sb_bq=CHUNK, length=CHUNK)
        sample_states.append(ss)
    p = [jnp.stack(s) for s in zip(*prompt_states)]
    s = [jnp.stack(s) for s in zip(*sample_states)]
    return (xp, xs, *p, *s)
```

```python
import functools

import jax
import jax.numpy as jnp
from jax import lax
from jax.experimental import pallas as pl
from jax.experimental.pallas import tpu as pltpu

F32 = jnp.float32
BF16 = jnp.bfloat16

D_MODEL = 1024
DEPTH = 2
CHUNK = 64
HA, HDA = 4, 64
HB, HDB = 4, 64
HC, HDC = 4, 128
DA, DB, DC = HA * HDA, HB * HDB, HC * HDC
BAND_CHUNKS = 8
BAND_W = (BAND_CHUNKS + 1) * CHUNK
REL_MAX = 128
CONV_W = 4
N_GROUPS = 4
E_PER_GROUP = 4
N_EXPERTS = N_GROUPS * E_PER_GROUP
D_EXPERT = 512
ALPHA = (2 * DEPTH) ** 0.25
LN_EPS = 1e-5

LANES = 128
Z_MAIN = 3 * DA + 3 * DB + 4 * DC
Z_COLS = Z_MAIN + LANES
SB_BLOCK = 128
SB_STAY_CUTOFF = 104.0
BAND_GROUP = 4
MOE_BLOCK = 128
VMEM_LIMIT = 48 * 1024 * 1024


def _cparams(*sem):
    return pltpu.CompilerParams(dimension_semantics=sem, vmem_limit_bytes=VMEM_LIMIT)


def _split3(x):
    hi = x.astype(BF16)
    r1 = x - hi.astype(F32)
    mid = r1.astype(BF16)
    lo = (r1 - mid.astype(F32)).astype(BF16)
    return hi, mid, lo


def _dot(a, b):
    return jnp.dot(a, b, preferred_element_type=F32)


def _dot_nt(a, b):
    return lax.dot_general(a, b, (((1,), (1,)), ((), ())), preferred_element_type=F32)


def _dot_tn(a, b):
    return lax.dot_general(a, b, (((0,), (0,)), ((), ())), preferred_element_type=F32)


def _log_sigmoid_pair(z):
    t = jnp.log1p(jnp.exp(-jnp.abs(z)))
    return -(jnp.maximum(-z, 0.0) + t), -(jnp.maximum(z, 0.0) + t)


def _stack_heads(q, n_heads, width):
    lane = lax.broadcasted_iota(jnp.int32, q.shape, 1)
    zero = jnp.zeros_like(q)
    return jnp.concatenate(
        [jnp.where((lane >= h * width) & (lane < (h + 1) * width), q, zero) for h in range(n_heads)], axis=0)


def _unstack_heads(acc, n_heads, width):
    rows = acc.shape[0] // n_heads
    lane = lax.broadcasted_iota(jnp.int32, (rows, n_heads * width), 1)
    out = acc[0:rows]
    for h in range(1, n_heads):
        out = jnp.where(lane >= h * width, acc[h * rows:(h + 1) * rows], out)
    return out


def _in_proj_kernel(x_ref, w_ref, qa, ka, va, kab, vab, qb, kb, vb, kbb, vbb, u, vcb, oc, g):
    xb = x_ref[...].astype(BF16)

    def proj(lo, width):
        return _dot(xb, w_ref[:, lo:lo + width])

    qa[...] = (proj(0, DA) * (HDA ** -0.5)).astype(BF16)
    for f32_ref, bf_ref, lo in ((ka, kab, DA), (va, vab, 2 * DA), (kb, kbb, 3 * DA + DB), (vb, vbb, 3 * DA + 2 * DB)):
        t = proj(lo, DA)
        f32_ref[...] = t
        bf_ref[...] = t.astype(BF16)
    qb[...] = (proj(3 * DA, DB) * (HDB ** -0.5)).astype(BF16)
    base = 3 * DA + 3 * DB
    for c in range(2):
        u[:, c * DC:(c + 1) * DC] = proj(base + c * DC, DC)
    vcb[...] = proj(base + 2 * DC, DC).astype(BF16)
    oc[...] = proj(base + 3 * DC, DC)
    g[...] = proj(Z_MAIN, LANES)


def _in_proj(x2d, w_bf16, tm):
    n = x2d.shape[0]
    widths_dtypes = [(DA, BF16), (DA, F32), (DA, F32), (DA, BF16), (DA, BF16),
                     (DB, BF16), (DB, F32), (DB, F32), (DB, BF16), (DB, BF16),
                     (2 * DC, F32), (DC, BF16), (DC, F32), (LANES, F32)]
    return pl.pallas_call(
        _in_proj_kernel,
        grid=(n // tm,),
        in_specs=[pl.BlockSpec((tm, D_MODEL), lambda i: (i, 0)),
                  pl.BlockSpec((D_MODEL, Z_COLS), lambda i: (0, 0))],
        out_specs=[pl.BlockSpec((tm, w), lambda i: (i, 0)) for w, _ in widths_dtypes],
        out_shape=[jax.ShapeDtypeStruct((n, w), dt) for w, dt in widths_dtypes],
        compiler_params=_cparams("parallel"),
        name="in_proj",
    )(x2d, w_bf16)


def _sb_kernel(q_ref, k_ref, v_ref, o_ref, carry_ref, acc_ref, *, past, bq, bk):
    i = pl.program_id(1)
    qs = _stack_heads(q_ref[...], HA, HDA)
    jj = lax.broadcasted_iota(jnp.int32, (bk, 2 * bk), 0)
    ss = lax.broadcasted_iota(jnp.int32, (bk, 2 * bk), 1)
    csum = jnp.where((ss >= bk) | (jj > ss), 1.0, 0.0).astype(BF16)
    kb_last = (past + (i + 1) * bq - 2) // bk

    def block(kb, diagonal):
        start = pl.multiple_of(kb * bk, bk)
        s = _dot_nt(qs, k_ref[pl.ds(start, bk), :])
        cost = jnp.maximum(s, 0.0) + jnp.log(1.0 + jnp.exp(-jnp.abs(s)))
        if diagonal:
            row = lax.broadcasted_iota(jnp.int32, (HA * bq, bk), 0) & (bq - 1)
            mask = (start + lax.broadcasted_iota(jnp.int32, (HA * bq, bk), 1)) < past + i * bq + row
            stay = jnp.where(mask, cost, 0.0)
        else:
            stay = cost
        hi = stay.astype(BF16)
        lo = (stay - hi.astype(F32)).astype(BF16)
        cs = _dot(hi, csum) + _dot(lo, csum)
        if diagonal:
            w = jnp.where(mask, jnp.exp(s - cost - cs[:, :bk]), 0.0)
            carry = cs[:, bk:]
            acc_ref[...] = _dot(w.astype(BF16), v_ref[pl.ds(start, bk), :])
        else:
            carry = carry_ref[...]
            w = jnp.exp(s - cost - (carry + cs[:, :bk]))
            carry = carry + cs[:, bk:]
            acc_ref[...] += _dot(w.astype(BF16), v_ref[pl.ds(start, bk), :])
        carry_ref[...] = carry
        return (jnp.min(carry) > SB_STAY_CUTOFF).astype(jnp.int32)

    done0 = block(kb_last, True)

    def cond(state):
        kb, done = state
        return jnp.logical_and(kb >= 0, done == 0)

    def body(state):
        kb, _ = state
        return kb - 1, block(kb, False)

    lax.while_loop(cond, body, (kb_last - 1, done0))
    o_ref[...] = _unstack_heads(acc_ref[...], HA, HDA).astype(o_ref.dtype)


def _sb_attention(q_bf16, k_bf16, v_bf16, *, batch, tq, past, bq):
    bk = SB_BLOCK
    nq = tq // bq
    tk = k_bf16.shape[1]
    assert tk % bk == 0 and (past + tq - 2) // bk < tk // bk
    assert past % bk == 0 and (bq == bk or (nq == 1 and bq <= bk))
    return pl.pallas_call(
        functools.partial(_sb_kernel, past=past, bq=bq, bk=bk),
        grid=(batch, nq),
        in_specs=[pl.BlockSpec((bq, DA), lambda b, i: (b * nq + i, 0)),
                  pl.BlockSpec((None, tk, DA), lambda b, i: (b, 0, 0)),
                  pl.BlockSpec((None, tk, DA), lambda b, i: (b, 0, 0))],
        out_specs=pl.BlockSpec((bq, DA), lambda b, i: (b * nq + i, 0)),
        out_shape=jax.ShapeDtypeStruct((batch * tq, DA), BF16),
        scratch_shapes=[pltpu.VMEM((HA * bq, bk), F32), pltpu.VMEM((HA * bq, DA), F32)],
        compiler_params=_cparams("parallel", "arbitrary"),
        name="sb_attention",
    )(q_bf16, k_bf16, v_bf16)


def _band_kernel(q_ref, *refs, n_kb, rows_per_pass):
    k_refs, v_refs = refs[:n_kb], refs[n_kb:2 * n_kb]
    bias_ref, o_ref, s_ref, p_ref = refs[2 * n_kb:]
    j = pl.program_id(1)
    qs = _stack_heads(q_ref[...], HB, HDB)
    kbs = k_refs[0].shape[0]
    for i in range(n_kb):
        s = _dot_nt(qs, k_refs[i][...]) + bias_ref[:, i * kbs:(i + 1) * kbs]
        if i < n_kb - 1:
            s = s + jnp.where(j + i < n_kb - 1, -jnp.inf, 0.0)
        s_ref[:, i * kbs:(i + 1) * kbs] = s
    for r in range(0, s_ref.shape[0], rows_per_pass):
        s = s_ref[r:r + rows_per_pass, :]
        e = jnp.exp(s - jnp.max(s, axis=-1, keepdims=True))
        p_ref[r:r + rows_per_pass, :] = (e * (1.0 / jnp.sum(e, axis=-1, keepdims=True))).astype(BF16)
    acc = _dot(p_ref[:, 0:kbs], v_refs[0][...])
    for i in range(1, n_kb):
        acc = acc + _dot(p_ref[:, i * kbs:(i + 1) * kbs], v_refs[i][...])
    o_ref[...] = _unstack_heads(acc, HB, HDB).astype(o_ref.dtype)


def _band_attention(q_bf16, k_bf16, v_bf16, bias, *, batch, steps, gq, n_kb, kbs):
    blocks_per_batch = k_bf16.shape[0] // (batch * kbs)
    kv_specs = [pl.BlockSpec((kbs, DB), functools.partial(
        lambda b, j, i: (b * blocks_per_batch + jnp.maximum(j + i - (n_kb - 1), 0), 0), i=i)) for i in range(n_kb)]
    return pl.pallas_call(
        functools.partial(_band_kernel, n_kb=n_kb, rows_per_pass=min(128, HB * gq)),
        grid=(batch, steps),
        in_specs=[pl.BlockSpec((gq, DB), lambda b, j: (b * steps + j, 0))] + kv_specs + kv_specs
        + [pl.BlockSpec((HB * gq, n_kb * kbs), lambda b, j: (0, 0))],
        out_specs=pl.BlockSpec((gq, DB), lambda b, j: (b * steps + j, 0)),
        out_shape=jax.ShapeDtypeStruct((batch * steps * gq, DB), BF16),
        scratch_shapes=[pltpu.VMEM((HB * gq, n_kb * kbs), F32), pltpu.VMEM((HB * gq, n_kb * kbs), BF16)],
        compiler_params=_cparams("parallel", "arbitrary"),
        name="band_attention",
    )(q_bf16, *([k_bf16] * n_kb), *([v_bf16] * n_kb), bias)


def _mlstm_kernel(u_ref, vc_ref, oc_ref, g_ref, convw_ref, convb_ref, bif_ref, ng_ref,
                  cn0_ref, m0_ref, conv0_ref, o_ref, cn_ref, m_ref, ubuf, *, length):
    L = length
    c = pl.program_id(1)

    @pl.when(c == 0)
    def _():
        cn_ref[...] = cn0_ref[...]
        m_ref[...] = m0_ref[...]
        ubuf[0:8, :] = conv0_ref[...]

    ubuf[8:8 + L, :] = u_ref[...]
    y = convb_ref[...]
    for w in range(CONV_W):
        y = y + ubuf[8 - (CONV_W - 1) + w:8 - (CONV_W - 1) + w + L, :] * convw_ref[w:w + 1, :]
    tail = ubuf[L:L + 8, :]
    ubuf[0:8, :] = tail
    qk = y * jax.nn.sigmoid(y)

    g = g_ref[...] + bif_ref[...]
    lf_all, _ = _log_sigmoid_pair(g)
    r_i = lax.broadcasted_iota(jnp.int32, (L, L), 0)
    c_i = lax.broadcasted_iota(jnp.int32, (L, L), 1)
    causal = c_i <= r_i
    tril = jnp.where(causal, 1.0, 0.0).astype(BF16)
    hi, mid, lo = _split3(lf_all)
    b_all = _dot(tril, hi) + _dot(tril, mid) + _dot(tril, lo)
    lane = lax.broadcasted_iota(jnp.int32, (L, LANES), 1)
    mixed = jnp.where(lane < HC, g, b_all)
    eye = jnp.where(lax.broadcasted_iota(jnp.int32, (16, LANES), 0)
                    == lax.broadcasted_iota(jnp.int32, (16, LANES), 1), 1.0, 0.0).astype(BF16)
    hi, mid, lo = _split3(mixed)
    rows = _dot_nt(eye, hi) + _dot_nt(eye, mid) + _dot_nt(eye, lo)
    ones_col = jnp.where(lax.broadcasted_iota(jnp.int32, (L, HDC), 1) == 0, 1.0, 0.0).astype(BF16)

    for h in range(HC):
        li_row = rows[h:h + 1, :]
        b_row = rows[HC + h:HC + h + 1, :]
        li_col = g[:, h:h + 1]
        b_col = b_all[:, HC + h:HC + h + 1]
        dmat = jnp.where(causal, b_col - b_row + li_row, -jnp.inf)
        m_prev = m_ref[h][0:1, 0:1]
        inter = b_col + m_prev
        m_t = jnp.maximum(inter, jnp.max(dmat, axis=-1, keepdims=True))
        w_intra = jnp.exp(dmat - m_t)
        w_inter = jnp.exp(inter - m_t)
        qh = qk[:, h * HDC:(h + 1) * HDC].astype(BF16)
        kf = qk[:, DC + h * HDC:DC + (h + 1) * HDC] * (HDC ** -0.5)
        vaug = jnp.concatenate([vc_ref[:, h * HDC:(h + 1) * HDC], ones_col], axis=1)
        s = _dot_nt(qh, kf.astype(BF16)) * w_intra
        cn = cn_ref[h]
        qcn = _dot(qh, cn.astype(BF16))
        sv = _dot(s.astype(BF16), vaug[:, :HDC])
        num = w_inter * qcn[:, :HDC] + sv
        den = w_inter * qcn[:, HDC:HDC + 1] + jnp.sum(s, axis=-1, keepdims=True)
        hh = num / jnp.maximum(jnp.abs(den), jnp.exp(-m_t))
        m_new = m_t[L - 1:L, :]
        decay = jnp.exp(inter[L - 1:L, :] - m_new)
        w_k = jnp.exp(b_col[L - 1:L, :] - b_col + li_col - m_new)
        cn_ref[h] = decay * cn + _dot_tn((kf * w_k).astype(BF16), vaug)
        m_ref[h] = jnp.broadcast_to(m_new, (8, LANES))
        mu = jnp.mean(hh, axis=-1, keepdims=True)
        dev = hh - mu
        var = jnp.mean(dev * dev, axis=-1, keepdims=True)
        hn = dev * lax.rsqrt(var + LN_EPS) * ng_ref[:, h * HDC:(h + 1) * HDC]
        gate = jax.nn.sigmoid(oc_ref[:, h * HDC:(h + 1) * HDC])
        o_ref[:, h * HDC:(h + 1) * HDC] = (gate * hn).astype(o_ref.dtype)


def _mlstm(u, vc_bf16, oc, gates, conv_w, conv_b, bif_pad, norm_g, cn0, m0, conv0, *, batch, t, length):
    nc = t // length
    row = lambda w: pl.BlockSpec((length, w), lambda b, c: (b * nc + c, 0))
    const = lambda shape: pl.BlockSpec(shape, lambda b, c: (0,) * len(shape))
    per_batch = lambda shape: pl.BlockSpec((None,) + shape, lambda b, c: (b,) + (0,) * len(shape))
    return pl.pallas_call(
        functools.partial(_mlstm_kernel, length=length),
        grid=(batch, nc),
        in_specs=[row(2 * DC), row(DC), row(DC), row(LANES),
                  const((CONV_W, 2 * DC)), const((1, 2 * DC)), const((1, LANES)), const((1, DC)),
                  per_batch((HC, HDC, 2 * HDC)), per_batch((HC, 8, LANES)), per_batch((8, 2 * DC))],
        out_specs=[row(DC), per_batch((HC, HDC, 2 * HDC)), per_batch((HC, 8, LANES))],
        out_shape=[jax.ShapeDtypeStruct((batch * t, DC), BF16),
                   jax.ShapeDtypeStruct((batch, HC, HDC, 2 * HDC), F32),
                   jax.ShapeDtypeStruct((batch, HC, 8, LANES), F32)],
        scratch_shapes=[pltpu.VMEM((8 + length, 2 * DC), F32)],
        compiler_params=_cparams("parallel", "arbitrary"),
        name="mlstm",
    )(u, vc_bf16, oc, gates, conv_w, conv_b, bif_pad, norm_g, cn0, m0, conv0)


def _layer_norm(y, g, b):
    mu = jnp.mean(y, axis=-1, keepdims=True)
    dev = y - mu
    var = jnp.mean(dev * dev, axis=-1, keepdims=True)
    return dev * lax.rsqrt(var + LN_EPS) * g + b


def _out_proj_kernel(x_ref, a_ref, b_ref, c_ref, w_ref, g_ref, beta_ref, o_ref):
    mix = (_dot(a_ref[...], w_ref[0:DA, :]) + _dot(b_ref[...], w_ref[DA:DA + DB, :])
           + _dot(c_ref[...], w_ref[DA + DB:, :]))
    o_ref[...] = _layer_norm(ALPHA * x_ref[...] + mix, g_ref[...], beta_ref[...])


def _out_proj(x2d, a, b, c, w_bf16, g, beta, tm):
    n = x2d.shape[0]
    row = lambda w: pl.BlockSpec((tm, w), lambda i: (i, 0))
    const = lambda shape: pl.BlockSpec(shape, lambda i: (0, 0))
    return pl.pallas_call(
        _out_proj_kernel,
        grid=(n // tm,),
        in_specs=[row(D_MODEL), row(DA), row(DB), row(DC), const((D_MODEL, D_MODEL)),
                  const((1, D_MODEL)), const((1, D_MODEL))],
        out_specs=row(D_MODEL),
        out_shape=jax.ShapeDtypeStruct((n, D_MODEL), F32),
        compiler_params=_cparams("parallel"),
        name="out_proj_ln",
    )(x2d, a, b, c, w_bf16, g, beta)


def _route(logits):
    lane = lax.broadcasted_iota(jnp.int32, logits.shape, 1)
    big = jnp.int32(LANES)
    is_g = lane < N_GROUPS
    lg = jnp.where(is_g, logits, -jnp.inf)
    g_max = jnp.max(lg, axis=-1, keepdims=True)
    g_sel = jnp.min(jnp.where(lg == g_max, lane, big), axis=-1, keepdims=True)
    p_g = 1.0 / jnp.sum(jnp.where(is_g, jnp.exp(logits - g_max), 0.0), axis=-1, keepdims=True)
    e_lo = N_GROUPS + g_sel * E_PER_GROUP
    in_group = (lane >= e_lo) & (lane < e_lo + E_PER_GROUP)
    le = jnp.where(in_group, logits, -jnp.inf)
    v1 = jnp.max(le, axis=-1, keepdims=True)
    i1 = jnp.min(jnp.where(le == v1, lane, big), axis=-1, keepdims=True)
    le2 = jnp.where(lane == i1, -jnp.inf, le)
    v2 = jnp.max(le2, axis=-1, keepdims=True)
    i2 = jnp.min(jnp.where(le2 == v2, lane, big), axis=-1, keepdims=True)
    e2 = jnp.exp(v2 - v1)
    tot = 1.0 + e2
    comb = jnp.where(lane == i1, p_g * (1.0 / tot), 0.0) + jnp.where(lane == i2, p_g * (e2 / tot), 0.0)
    return comb, g_sel


def _moe_kernel(x_ref, rw_ref, rb_ref, wg_ref, wu_ref, wd_ref, g_ref, beta_ref, o_ref,
                xs_ref, lo_ref, combs_ref, acc_ref, pos_ref, tab_ref):
    e = pl.program_id(1)
    tm = x_ref.shape[0]
    n_sorted = xs_ref.shape[0]
    blk = MOE_BLOCK

    @pl.when(e == 0)
    def _():
        xb = x_ref[...].astype(BF16)
        comb, g_sel = _route(_dot(xb, rw_ref[...]) + rb_ref[...])
        lane = lax.broadcasted_iota(jnp.int32, (tm, LANES), 1)
        onehot = jnp.where(lane == g_sel, 1.0, 0.0)
        onehot_bf = onehot.astype(BF16)
        col = lax.broadcasted_iota(jnp.int32, (blk, tm), 1)
        row = lax.broadcasted_iota(jnp.int32, (blk, tm), 0)
        cum = jnp.concatenate(
            [_dot(jnp.where(col <= row + r, 1.0, 0.0).astype(BF16), onehot_bf) for r in range(0, tm, blk)], axis=0)
        count = cum[tm - 1:tm, :]
        padded = jnp.floor((count + (blk - 1)) * (1.0 / blk)) * blk
        lane1 = lax.broadcasted_iota(jnp.int32, (1, LANES), 1)
        offsets = jnp.zeros((1, LANES), F32)
        for grp in range(N_GROUPS):
            off = jnp.sum(jnp.where(lane1 < grp, padded, 0.0), axis=-1, keepdims=True)
            n_blocks = jnp.sum(jnp.where(lane1 == grp, padded, 0.0), axis=-1, keepdims=True) * (1.0 / blk)
            offsets = jnp.where(lane1 == grp, off, offsets)
            tab_ref[grp] = off.astype(jnp.int32)[0, 0]
            tab_ref[N_GROUPS + grp] = n_blocks.astype(jnp.int32)[0, 0]
        pos = jnp.sum(onehot * (offsets + cum), axis=-1, keepdims=True) - 1.0
        pos_ref[...] = pos.astype(jnp.int32)
        pos_hi = jnp.floor(pos * (1.0 / 256.0))
        digits = jnp.where(lane == 0, pos_hi, jnp.where(lane == 1, pos - 256.0 * pos_hi, 0.0)).astype(BF16)
        eye = jnp.where(lax.broadcasted_iota(jnp.int32, (16, LANES), 0)
                        == lax.broadcasted_iota(jnp.int32, (16, LANES), 1), 1.0, 0.0).astype(BF16)
        dig_rows = _dot_nt(eye, digits)
        pos_row = (dig_rows[0:1, :] * 256.0 + dig_rows[1:2, :]).astype(jnp.int32)
        c_hi, c_mid, c_lo = _split3(comb)
        for r in range(0, n_sorted, blk):
            perm = jnp.where(row == pos_row - r, 1.0, 0.0).astype(BF16)
            xs_ref[r:r + blk, :] = _dot(perm, xb).astype(BF16)
            combs_ref[r:r + blk, :] = _dot(perm, c_hi) + _dot(perm, c_mid) + _dot(perm, c_lo)
        acc_ref[...] = jnp.zeros_like(acc_ref)

    grp = e // E_PER_GROUP
    start = tab_ref[grp]
    lane_b = lax.broadcasted_iota(jnp.int32, (blk, LANES), 1)

    def expert_block(b, carry):
        r0 = pl.multiple_of(start + b * blk, blk)
        xsb = xs_ref[pl.ds(r0, blk), :]
        gate = _dot(xsb, wg_ref[...])
        h = gate * jax.nn.sigmoid(gate) * _dot(xsb, wu_ref[...])
        y = _dot(h.astype(BF16), wd_ref[...])
        comb_e = jnp.sum(jnp.where(lane_b == N_GROUPS + e, combs_ref[pl.ds(r0, blk), :], 0.0),
                         axis=-1, keepdims=True)
        acc_ref[pl.ds(r0, blk), :] += comb_e * y
        return carry

    lax.fori_loop(0, tab_ref[N_GROUPS + grp], expert_block, 0)

    @pl.when(e == N_EXPERTS - 1)
    def _():
        for r in range(0, n_sorted, blk):
            a = acc_ref[r:r + blk, :]
            hi = a.astype(BF16)
            xs_ref[r:r + blk, :] = hi
            lo_ref[r:r + blk, :] = (a - hi.astype(F32)).astype(BF16)
        lane_s = lax.broadcasted_iota(jnp.int32, (blk, n_sorted), 1)
        for r in range(0, tm, blk):
            back = jnp.where(lane_s == pos_ref[r:r + blk, :], 1.0, 0.0).astype(BF16)
            y = _dot(back, xs_ref[...]) + _dot(back, lo_ref[...])
            o_ref[r:r + blk, :] = _layer_norm(ALPHA * x_ref[r:r + blk, :] + y, g_ref[...], beta_ref[...])


def _moe(x2d, rw_bf16, rb, wg, wu, wd, g, beta, tm):
    n = x2d.shape[0]
    n_sorted = tm + N_GROUPS * MOE_BLOCK
    assert tm % MOE_BLOCK == 0 and n_sorted < 256 * 256
    const = lambda shape: pl.BlockSpec(shape, lambda i, e: (0, 0))
    return pl.pallas_call(
        _moe_kernel,
        grid=(n // tm, N_EXPERTS),
        in_specs=[pl.BlockSpec((tm, D_MODEL), lambda i, e: (i, 0)),
                  const((D_MODEL, LANES)), const((1, LANES)),
                  pl.BlockSpec((None, D_MODEL, D_EXPERT), lambda i, e: (e, 0, 0)),
                  pl.BlockSpec((None, D_MODEL, D_EXPERT), lambda i, e: (e, 0, 0)),
                  pl.BlockSpec((None, D_EXPERT, D_MODEL), lambda i, e: (e, 0, 0)),
                  const((1, D_MODEL)), const((1, D_MODEL))],
        out_specs=pl.BlockSpec((tm, D_MODEL), lambda i, e: (i, 0)),
        out_shape=jax.ShapeDtypeStruct((n, D_MODEL), F32),
        scratch_shapes=[pltpu.VMEM((n_sorted, D_MODEL), BF16), pltpu.VMEM((n_sorted, D_MODEL), BF16),
                        pltpu.VMEM((n_sorted, LANES), F32), pltpu.VMEM((n_sorted, D_MODEL), F32),
                        pltpu.VMEM((tm, 1), jnp.int32), pltpu.SMEM((2 * N_GROUPS,), jnp.int32)],
        compiler_params=_cparams("parallel", "arbitrary"),
        name="moe_ln",
    )(x2d, rw_bf16, rb, wg, wu, wd, g, beta)


def _band_bias(rel_table):
    n_clipped = BAND_W - 1 - REL_MAX
    ext = jnp.concatenate([rel_table[:, REL_MAX - (CHUNK - 1):],
                           jnp.broadcast_to(rel_table[:, -1:], (HB, n_clipped))], axis=1)
    rev = ext[:, ::-1]
    return jnp.stack([rev[:, CHUNK - 1 - i:CHUNK - 1 - i + BAND_W] for i in range(CHUNK)], axis=1).astype(F32)


def _band_bias_grouped(bias):
    g_n = BAND_GROUP
    tabs = [jnp.pad(bias, ((0, 0), (0, 0), (g * CHUNK, (g_n - 1 - g) * CHUNK)), constant_values=-jnp.inf)
            for g in range(g_n)]
    return jnp.stack(tabs, axis=1).reshape(HB * g_n * CHUNK, (g_n + BAND_CHUNKS) * CHUNK)


def _layer_weights(l, w_in, b_if, conv_w, conv_b, rel_table, c_norm_g, w_out, ln1_g, ln1_b,
                   router_g_w, router_g_b, router_e_w, router_e_b, exp_w_gate, exp_w_up, exp_w_down,
                   ln2_g, ln2_b):
    pad_cols = lambda a: jnp.pad(a, ((0, 0), (0, LANES - a.shape[1])))
    bias = _band_bias(rel_table[l])
    return dict(
        w_in=jnp.concatenate([w_in[l, :, :Z_MAIN], pad_cols(w_in[l, :, Z_MAIN:])], axis=1).astype(BF16),
        bif=pad_cols(b_if[l][None, :]),
        conv_w=conv_w[l], conv_b=conv_b[l][None, :],
        bias=bias.reshape(HB * CHUNK, BAND_W), bias_grouped=_band_bias_grouped(bias),
        norm_g=c_norm_g[l].reshape(1, DC),
        w_out=w_out[l].astype(BF16),
        ln1_g=ln1_g[l][None, :], ln1_b=ln1_b[l][None, :],
        rw=pad_cols(jnp.concatenate([router_g_w[l], router_e_w[l]], axis=1)).astype(BF16),
        rb=pad_cols(jnp.concatenate([router_g_b[l], router_e_b[l]])[None, :]),
        wg=exp_w_gate[l].astype(BF16), wu=exp_w_up[l].astype(BF16), wd=exp_w_down[l].astype(BF16),
        ln2_g=ln2_g[l][None, :], ln2_b=ln2_b[l][None, :],
    )


def _layer(x, w, cache, *, tm, tm_moe, sb_bq, length):
    batch, t, _ = x.shape
    n = batch * t
    tm, tm_moe = min(tm, n), min(tm_moe, n)
    x2d = x.reshape(n, D_MODEL)
    qa, ka, va, ka_bf, va_bf, qb, kb, vb, kb_bf, vb_bf, u, vc_bf, oc, gates = _in_proj(x2d, w["w_in"], tm)
    per_batch = lambda a: a.reshape(batch, t, a.shape[-1])

    if cache is None:
        past = 0
        sb_k, sb_v = per_batch(ka_bf), per_batch(va_bf)
        gq = BAND_GROUP * CHUNK
        assert t % gq == 0 and BAND_CHUNKS * CHUNK % gq == 0
        band = dict(k=kb_bf, v=vb_bf, bias=w["bias_grouped"], steps=t // gq, gq=gq,
                    n_kb=BAND_CHUNKS * CHUNK // gq + 1, kbs=gq)
        cn0 = jnp.zeros((batch, HC, HDC, 2 * HDC), F32)
        m0 = jnp.zeros((batch, HC, 8, LANES), F32)
        conv0 = jnp.zeros((batch, 8, 2 * DC), F32)
    else:
        a_k, a_v, b_k, b_v, c0, n0, m_init, conv_init = cache
        past = a_k.shape[1]
        tk_pad = -(past + t) % SB_BLOCK
        sb_cat = lambda old, new: jnp.pad(
            jnp.concatenate([old.reshape(batch, past, DA).astype(BF16), per_batch(new)], axis=1),
            ((0, 0), (0, tk_pad), (0, 0)))
        sb_k, sb_v = sb_cat(a_k, ka_bf), sb_cat(a_v, va_bf)
        hist = b_k.shape[1]
        assert hist == BAND_CHUNKS * CHUNK and t == CHUNK and past % CHUNK == 0 and past >= hist
        band_cat = lambda old, new: jnp.concatenate(
            [old.reshape(batch, hist, DB).astype(BF16), per_batch(new)], axis=1).reshape(batch * BAND_W, DB)
        band = dict(k=band_cat(b_k, kb_bf), v=band_cat(b_v, vb_bf), bias=w["bias"], steps=1, gq=CHUNK,
                    n_kb=1, kbs=BAND_W)
        cn0 = jnp.concatenate([c0, n0[..., None], jnp.zeros((batch, HC, HDC, HDC - 1), F32)], axis=-1)
        m0 = jnp.broadcast_to(m_init[:, :, None, None], (batch, HC, 8, LANES))
        conv0 = jnp.pad(conv_init, ((0, 0), (8 - (CONV_W - 1), 0), (0, 0)))

    a_out = _sb_attention(qa, sb_k, sb_v, batch=batch, tq=t, past=past, bq=sb_bq)
    b_out = _band_attention(qb, band["k"], band["v"], band["bias"], batch=batch, steps=band["steps"],
                            gq=band["gq"], n_kb=band["n_kb"], kbs=band["kbs"])
    c_out, cn1, m1 = _mlstm(u, vc_bf, oc, gates, w["conv_w"], w["conv_b"], w["bif"], w["norm_g"],
                            cn0, m0, conv0, batch=batch, t=t, length=length)
    x1 = _out_proj(x2d, a_out, b_out, c_out, w["w_out"], w["ln1_g"], w["ln1_b"], tm)
    x2 = _moe(x1, w["rw"], w["rb"], w["wg"], w["wu"], w["wd"], w["ln2_g"], w["ln2_b"], tm_moe)
    state = (ka.reshape(batch, t, HA, HDA), va.reshape(batch, t, HA, HDA),
             kb.reshape(batch, t, HB, HDB), vb.reshape(batch, t, HB, HDB),
             cn1[..., :HDC], cn1[..., HDC], m1[:, :, 0, 0], per_batch(u)[:, t - (CONV_W - 1):])
    return x2.reshape(batch, t, D_MODEL), state


def kernel(x_prompt, x_sample, cache_a_k, cache_a_v, cache_b_k, cache_b_v, state_c_C, state_c_n,
           state_c_m, state_c_conv, w_in, b_if, conv_w, conv_b, rel_table, c_norm_g, w_out, ln1_g, ln1_b,
           router_g_w, router_g_b, router_e_w, router_e_b, exp_w_gate, exp_w_up, exp_w_down, ln2_g, ln2_b):
    band_keep = cache_b_k.shape[2]
    xp, xs = x_prompt, x_sample
    prompt_states, sample_states = [], []
    for l in range(DEPTH):
        w = _layer_weights(l, w_in, b_if, conv_w, conv_b, rel_table, c_norm_g, w_out, ln1_g, ln1_b,
                           router_g_w, router_g_b, router_e_w, router_e_b, exp_w_gate, exp_w_up,
                           exp_w_down, ln2_g, ln2_b)
        xp, sp = _layer(xp, w, None, tm=512, tm_moe=1024, sb_bq=SB_BLOCK, length=2 * CHUNK)
        ak, av, bk, bv, cc, cn, cm, cconv = sp
        prompt_states.append((ak, av, bk[:, -band_keep:], bv[:, -band_keep:], cc, cn, cm, cconv))
        cache_l = (cache_a_k[l], cache_a_v[l], cache_b_k[l], cache_b_v[l],
                   state_c_C[l], state_c_n[l], state_c_m[l], state_c_conv[l])
        xs, ss = _layer(xs, w, cache_l, tm=512, tm_moe=512, sb_bq=CHUNK, length=CHUNK)
        sample_states.append(ss)
    p = [jnp.stack(s) for s in zip(*prompt_states)]
    s = [jnp.stack(s) for s in zip(*sample_states)]
    return (xp, xs, *p, *s)
```

```python
import functools

import jax
import jax.numpy as jnp
from jax import lax
from jax.experimental import pallas as pl
from jax.experimental.pallas import tpu as pltpu

F32 = jnp.float32
BF16 = jnp.bfloat16

D_MODEL = 1024
DEPTH = 2
CHUNK = 64
HA, HDA = 4, 64
HB, HDB = 4, 64
HC, HDC = 4, 128
DA, DB, DC = HA * HDA, HB * HDB, HC * HDC
BAND_CHUNKS = 8
BAND_W = (BAND_CHUNKS + 1) * CHUNK
REL_MAX = 128
CONV_W = 4
N_GROUPS = 4
E_PER_GROUP = 4
N_EXPERTS = N_GROUPS * E_PER_GROUP
D_EXPERT = 512
ALPHA = (2 * DEPTH) ** 0.25
LN_EPS = 1e-5

LANES = 128
Z_MAIN = 3 * DA + 3 * DB + 4 * DC
Z_COLS = Z_MAIN + LANES
SB_BLOCK = 128
SB_STAY_CUTOFF = 104.0
BAND_GROUP = 4
MOE_BLOCK = 128
MOE_EXPERTS_PER_STEP = 2
VMEM_LIMIT = 48 * 1024 * 1024


def _cparams(*sem):
    return pltpu.CompilerParams(dimension_semantics=sem, vmem_limit_bytes=VMEM_LIMIT)


def _split3(x):
    hi = x.astype(BF16)
    r1 = x - hi.astype(F32)
    mid = r1.astype(BF16)
    lo = (r1 - mid.astype(F32)).astype(BF16)
    return hi, mid, lo


def _dot(a, b):
    return jnp.dot(a, b, preferred_element_type=F32)


def _dot_nt(a, b):
    return lax.dot_general(a, b, (((1,), (1,)), ((), ())), preferred_element_type=F32)


def _dot_tn(a, b):
    return lax.dot_general(a, b, (((0,), (0,)), ((), ())), preferred_element_type=F32)


def _log_sigmoid_pair(z):
    t = jnp.log1p(jnp.exp(-jnp.abs(z)))
    return -(jnp.maximum(-z, 0.0) + t), -(jnp.maximum(z, 0.0) + t)


def _stack_heads(q, n_heads, width):
    lane = lax.broadcasted_iota(jnp.int32, q.shape, 1)
    zero = jnp.zeros_like(q)
    return jnp.concatenate(
        [jnp.where((lane >= h * width) & (lane < (h + 1) * width), q, zero) for h in range(n_heads)], axis=0)


def _unstack_heads(acc, n_heads, width):
    rows = acc.shape[0] // n_heads
    lane = lax.broadcasted_iota(jnp.int32, (rows, n_heads * width), 1)
    out = acc[0:rows]
    for h in range(1, n_heads):
        out = jnp.where(lane >= h * width, acc[h * rows:(h + 1) * rows], out)
    return out


def _in_proj_kernel(x_ref, w_ref, qa, ka, va, kab, vab, qb, kb, vb, kbb, vbb, u, vcb, oc, g):
    xb = x_ref[...].astype(BF16)

    def proj(lo, width):
        return _dot(xb, w_ref[:, lo:lo + width])

    qa[...] = (proj(0, DA) * (HDA ** -0.5)).astype(BF16)
    for f32_ref, bf_ref, lo in ((ka, kab, DA), (va, vab, 2 * DA), (kb, kbb, 3 * DA + DB), (vb, vbb, 3 * DA + 2 * DB)):
        t = proj(lo, DA)
        f32_ref[...] = t
        bf_ref[...] = t.astype(BF16)
    qb[...] = (proj(3 * DA, DB) * (HDB ** -0.5)).astype(BF16)
    base = 3 * DA + 3 * DB
    for c in range(2):
        u[:, c * DC:(c + 1) * DC] = proj(base + c * DC, DC)
    vcb[...] = proj(base + 2 * DC, DC).astype(BF16)
    oc[...] = proj(base + 3 * DC, DC)
    g[...] = proj(Z_MAIN, LANES)


def _in_proj(x2d, w_bf16, tm):
    n = x2d.shape[0]
    widths_dtypes = [(DA, BF16), (DA, F32), (DA, F32), (DA, BF16), (DA, BF16),
                     (DB, BF16), (DB, F32), (DB, F32), (DB, BF16), (DB, BF16),
                     (2 * DC, F32), (DC, BF16), (DC, F32), (LANES, F32)]
    return pl.pallas_call(
        _in_proj_kernel,
        grid=(n // tm,),
        in_specs=[pl.BlockSpec((tm, D_MODEL), lambda i: (i, 0)),
                  pl.BlockSpec((D_MODEL, Z_COLS), lambda i: (0, 0))],
        out_specs=[pl.BlockSpec((tm, w), lambda i: (i, 0)) for w, _ in widths_dtypes],
        out_shape=[jax.ShapeDtypeStruct((n, w), dt) for w, dt in widths_dtypes],
        compiler_params=_cparams("parallel"),
        name="in_proj",
    )(x2d, w_bf16)


def _sb_kernel(q_ref, k_ref, v_ref, o_ref, carry_ref, acc_ref, *, past, bq, bk):
    i = pl.program_id(1)
    qs = _stack_heads(q_ref[...], HA, HDA)
    jj = lax.broadcasted_iota(jnp.int32, (bk, 2 * bk), 0)
    ss = lax.broadcasted_iota(jnp.int32, (bk, 2 * bk), 1)
    csum = jnp.where((ss >= bk) | (jj > ss), 1.0, 0.0).astype(BF16)
    csum2 = jnp.concatenate([csum, csum], axis=0)
    kb_last = (past + (i + 1) * bq - 2) // bk

    def block(kb, diagonal):
        start = pl.multiple_of(kb * bk, bk)
        s = _dot_nt(qs, k_ref[pl.ds(start, bk), :])
        cost = jnp.maximum(s, 0.0) + jnp.log(1.0 + jnp.exp(-jnp.abs(s)))
        if diagonal:
            row = lax.broadcasted_iota(jnp.int32, (HA * bq, bk), 0) & (bq - 1)
            mask = (start + lax.broadcasted_iota(jnp.int32, (HA * bq, bk), 1)) < past + i * bq + row
            stay = jnp.where(mask, cost, 0.0)
        else:
            stay = cost
        hi = stay.astype(BF16)
        lo = (stay - hi.astype(F32)).astype(BF16)
        cs = _dot(jnp.concatenate([hi, lo], axis=1), csum2)
        if diagonal:
            w = jnp.where(mask, jnp.exp(s - cost - cs[:, :bk]), 0.0)
            carry = cs[:, bk:]
            acc_ref[...] = _dot(w.astype(BF16), v_ref[pl.ds(start, bk), :])
        else:
            carry = carry_ref[...]
            w = jnp.exp(s - cost - (carry + cs[:, :bk]))
            carry = carry + cs[:, bk:]
            acc_ref[...] += _dot(w.astype(BF16), v_ref[pl.ds(start, bk), :])
        carry_ref[...] = carry
        return (jnp.min(carry) > SB_STAY_CUTOFF).astype(jnp.int32)

    done0 = block(kb_last, True)

    def cond(state):
        kb, done = state
        return jnp.logical_and(kb >= 0, done == 0)

    def body(state):
        kb, _ = state
        return kb - 1, block(kb, False)

    lax.while_loop(cond, body, (kb_last - 1, done0))
    o_ref[...] = _unstack_heads(acc_ref[...], HA, HDA).astype(o_ref.dtype)


def _sb_attention(q_bf16, k_bf16, v_bf16, *, batch, tq, past, bq):
    bk = SB_BLOCK
    nq = tq // bq
    tk = k_bf16.shape[1]
    assert tk % bk == 0 and (past + tq - 2) // bk < tk // bk
    assert past % bk == 0 and (bq == bk or (nq == 1 and bq <= bk))
    return pl.pallas_call(
        functools.partial(_sb_kernel, past=past, bq=bq, bk=bk),
        grid=(batch, nq),
        in_specs=[pl.BlockSpec((bq, DA), lambda b, i: (b * nq + i, 0)),
                  pl.BlockSpec((None, tk, DA), lambda b, i: (b, 0, 0)),
                  pl.BlockSpec((None, tk, DA), lambda b, i: (b, 0, 0))],
        out_specs=pl.BlockSpec((bq, DA), lambda b, i: (b * nq + i, 0)),
        out_shape=jax.ShapeDtypeStruct((batch * tq, DA), BF16),
        scratch_shapes=[pltpu.VMEM((HA * bq, bk), F32), pltpu.VMEM((HA * bq, DA), F32)],
        compiler_params=_cparams("parallel", "arbitrary"),
        name="sb_attention",
    )(q_bf16, k_bf16, v_bf16)


def _band_kernel(q_ref, *refs, n_kb, rows_per_pass):
    k_refs, v_refs = refs[:n_kb], refs[n_kb:2 * n_kb]
    bias_ref, o_ref, s_ref, p_ref = refs[2 * n_kb:]
    j = pl.program_id(1)
    qs = _stack_heads(q_ref[...], HB, HDB)
    kbs = k_refs[0].shape[0]
    for i in range(n_kb):
        s = _dot_nt(qs, k_refs[i][...]) + bias_ref[:, i * kbs:(i + 1) * kbs]
        if i < n_kb - 1:
            s = s + jnp.where(j + i < n_kb - 1, -jnp.inf, 0.0)
        s_ref[:, i * kbs:(i + 1) * kbs] = s
    for r in range(0, s_ref.shape[0], rows_per_pass):
        s = s_ref[r:r + rows_per_pass, :]
        e = jnp.exp(s - jnp.max(s, axis=-1, keepdims=True))
        p_ref[r:r + rows_per_pass, :] = (e * (1.0 / jnp.sum(e, axis=-1, keepdims=True))).astype(BF16)
    acc = _dot(p_ref[:, 0:kbs], v_refs[0][...])
    for i in range(1, n_kb):
        acc = acc + _dot(p_ref[:, i * kbs:(i + 1) * kbs], v_refs[i][...])
    o_ref[...] = _unstack_heads(acc, HB, HDB).astype(o_ref.dtype)


def _band_attention(q_bf16, k_bf16, v_bf16, bias, *, batch, steps, gq, n_kb, kbs):
    blocks_per_batch = k_bf16.shape[0] // (batch * kbs)
    kv_specs = [pl.BlockSpec((kbs, DB), functools.partial(
        lambda b, j, i: (b * blocks_per_batch + jnp.maximum(j + i - (n_kb - 1), 0), 0), i=i)) for i in range(n_kb)]
    return pl.pallas_call(
        functools.partial(_band_kernel, n_kb=n_kb, rows_per_pass=min(128, HB * gq)),
        grid=(batch, steps),
        in_specs=[pl.BlockSpec((gq, DB), lambda b, j: (b * steps + j, 0))] + kv_specs + kv_specs
        + [pl.BlockSpec((HB * gq, n_kb * kbs), lambda b, j: (0, 0))],
        out_specs=pl.BlockSpec((gq, DB), lambda b, j: (b * steps + j, 0)),
        out_shape=jax.ShapeDtypeStruct((batch * steps * gq, DB), BF16),
        scratch_shapes=[pltpu.VMEM((HB * gq, n_kb * kbs), F32), pltpu.VMEM((HB * gq, n_kb * kbs), BF16)],
        compiler_params=_cparams("parallel", "arbitrary"),
        name="band_attention",
    )(q_bf16, *([k_bf16] * n_kb), *([v_bf16] * n_kb), bias)


def _mlstm_kernel(u_ref, vc_ref, oc_ref, g_ref, convw_ref, convb_ref, bif_ref, ng_ref,
                  cn0_ref, m0_ref, conv0_ref, o_ref, cn_ref, m_ref, ubuf, *, length):
    L = length
    c = pl.program_id(1)

    @pl.when(c == 0)
    def _():
        cn_ref[...] = cn0_ref[...]
        m_ref[...] = m0_ref[...]
        ubuf[0:8, :] = conv0_ref[...]

    ubuf[8:8 + L, :] = u_ref[...]
    y = convb_ref[...]
    for w in range(CONV_W):
        y = y + ubuf[8 - (CONV_W - 1) + w:8 - (CONV_W - 1) + w + L, :] * convw_ref[w:w + 1, :]
    tail = ubuf[L:L + 8, :]
    ubuf[0:8, :] = tail
    qk = y * jax.nn.sigmoid(y)

    g = g_ref[...] + bif_ref[...]
    lf_all, _ = _log_sigmoid_pair(g)
    r_i = lax.broadcasted_iota(jnp.int32, (L, L), 0)
    c_i = lax.broadcasted_iota(jnp.int32, (L, L), 1)
    causal = c_i <= r_i
    tril = jnp.where(causal, 1.0, 0.0).astype(BF16)
    hi, mid, lo = _split3(lf_all)
    b_all = _dot(tril, hi) + _dot(tril, mid) + _dot(tril, lo)
    lane = lax.broadcasted_iota(jnp.int32, (L, LANES), 1)
    mixed = jnp.where(lane < HC, g, b_all)
    eye = jnp.where(lax.broadcasted_iota(jnp.int32, (16, LANES), 0)
                    == lax.broadcasted_iota(jnp.int32, (16, LANES), 1), 1.0, 0.0).astype(BF16)
    hi, mid, lo = _split3(mixed)
    rows = _dot_nt(eye, hi) + _dot_nt(eye, mid) + _dot_nt(eye, lo)
    ones_col = jnp.where(lax.broadcasted_iota(jnp.int32, (L, HDC), 1) == 0, 1.0, 0.0).astype(BF16)

    for h in range(HC):
        li_row = rows[h:h + 1, :]
        b_row = rows[HC + h:HC + h + 1, :]
        li_col = g[:, h:h + 1]
        b_col = b_all[:, HC + h:HC + h + 1]
        dmat = jnp.where(causal, b_col - b_row + li_row, -jnp.inf)
        m_prev = m_ref[h][0:1, 0:1]
        inter = b_col + m_prev
        m_t = jnp.maximum(inter, jnp.max(dmat, axis=-1, keepdims=True))
        w_intra = jnp.exp(dmat - m_t)
        w_inter = jnp.exp(inter - m_t)
        qh = qk[:, h * HDC:(h + 1) * HDC].astype(BF16)
        kf = qk[:, DC + h * HDC:DC + (h + 1) * HDC] * (HDC ** -0.5)
        vaug = jnp.concatenate([vc_ref[:, h * HDC:(h + 1) * HDC], ones_col], axis=1)
        s = _dot_nt(qh, kf.astype(BF16)) * w_intra
        cn = cn_ref[h]
        qcn = _dot(qh, cn.astype(BF16))
        sv = _dot(s.astype(BF16), vaug[:, :HDC])
        num = w_inter * qcn[:, :HDC] + sv
        den = w_inter * qcn[:, HDC:HDC + 1] + jnp.sum(s, axis=-1, keepdims=True)
        hh = num / jnp.maximum(jnp.abs(den), jnp.exp(-m_t))
        m_new = m_t[L - 1:L, :]
        decay = jnp.exp(inter[L - 1:L, :] - m_new)
        w_k = jnp.exp(b_col[L - 1:L, :] - b_col + li_col - m_new)
        cn_ref[h] = decay * cn + _dot_tn((kf * w_k).astype(BF16), vaug)
        m_ref[h] = jnp.broadcast_to(m_new, (8, LANES))
        mu = jnp.mean(hh, axis=-1, keepdims=True)
        dev = hh - mu
        var = jnp.mean(dev * dev, axis=-1, keepdims=True)
        hn = dev * lax.rsqrt(var + LN_EPS) * ng_ref[:, h * HDC:(h + 1) * HDC]
        gate = jax.nn.sigmoid(oc_ref[:, h * HDC:(h + 1) * HDC])
        o_ref[:, h * HDC:(h + 1) * HDC] = (gate * hn).astype(o_ref.dtype)


def _mlstm(u, vc_bf16, oc, gates, conv_w, conv_b, bif_pad, norm_g, cn0, m0, conv0, *, batch, t, length):
    nc = t // length
    row = lambda w: pl.BlockSpec((length, w), lambda b, c: (b * nc + c, 0))
    const = lambda shape: pl.BlockSpec(shape, lambda b, c: (0,) * len(shape))
    per_batch = lambda shape: pl.BlockSpec((None,) + shape, lambda b, c: (b,) + (0,) * len(shape))
    return pl.pallas_call(
        functools.partial(_mlstm_kernel, length=length),
        grid=(batch, nc),
        in_specs=[row(2 * DC), row(DC), row(DC), row(LANES),
                  const((CONV_W, 2 * DC)), const((1, 2 * DC)), const((1, LANES)), const((1, DC)),
                  per_batch((HC, HDC, 2 * HDC)), per_batch((HC, 8, LANES)), per_batch((8, 2 * DC))],
        out_specs=[row(DC), per_batch((HC, HDC, 2 * HDC)), per_batch((HC, 8, LANES))],
        out_shape=[jax.ShapeDtypeStruct((batch * t, DC), BF16),
                   jax.ShapeDtypeStruct((batch, HC, HDC, 2 * HDC), F32),
                   jax.ShapeDtypeStruct((batch, HC, 8, LANES), F32)],
        scratch_shapes=[pltpu.VMEM((8 + length, 2 * DC), F32)],
        compiler_params=_cparams("parallel", "arbitrary"),
        name="mlstm",
    )(u, vc_bf16, oc, gates, conv_w, conv_b, bif_pad, norm_g, cn0, m0, conv0)


def _layer_norm(y, g, b):
    mu = jnp.mean(y, axis=-1, keepdims=True)
    dev = y - mu
    var = jnp.mean(dev * dev, axis=-1, keepdims=True)
    return dev * lax.rsqrt(var + LN_EPS) * g + b


def _out_proj_kernel(x_ref, a_ref, b_ref, c_ref, w_ref, g_ref, beta_ref, o_ref):
    mix = (_dot(a_ref[...], w_ref[0:DA, :]) + _dot(b_ref[...], w_ref[DA:DA + DB, :])
           + _dot(c_ref[...], w_ref[DA + DB:, :]))
    o_ref[...] = _layer_norm(ALPHA * x_ref[...] + mix, g_ref[...], beta_ref[...])


def _out_proj(x2d, a, b, c, w_bf16, g, beta, tm):
    n = x2d.shape[0]
    row = lambda w: pl.BlockSpec((tm, w), lambda i: (i, 0))
    const = lambda shape: pl.BlockSpec(shape, lambda i: (0, 0))
    return pl.pallas_call(
        _out_proj_kernel,
        grid=(n // tm,),
        in_specs=[row(D_MODEL), row(DA), row(DB), row(DC), const((D_MODEL, D_MODEL)),
                  const((1, D_MODEL)), const((1, D_MODEL))],
        out_specs=row(D_MODEL),
        out_shape=jax.ShapeDtypeStruct((n, D_MODEL), F32),
        compiler_params=_cparams("parallel"),
        name="out_proj_ln",
    )(x2d, a, b, c, w_bf16, g, beta)


def _route(logits):
    lane = lax.broadcasted_iota(jnp.int32, logits.shape, 1)
    big = jnp.int32(LANES)
    is_g = lane < N_GROUPS
    lg = jnp.where(is_g, logits, -jnp.inf)
    g_max = jnp.max(lg, axis=-1, keepdims=True)
    g_sel = jnp.min(jnp.where(lg == g_max, lane, big), axis=-1, keepdims=True)
    p_g = 1.0 / jnp.sum(jnp.where(is_g, jnp.exp(logits - g_max), 0.0), axis=-1, keepdims=True)
    e_lo = N_GROUPS + g_sel * E_PER_GROUP
    in_group = (lane >= e_lo) & (lane < e_lo + E_PER_GROUP)
    le = jnp.where(in_group, logits, -jnp.inf)
    v1 = jnp.max(le, axis=-1, keepdims=True)
    i1 = jnp.min(jnp.where(le == v1, lane, big), axis=-1, keepdims=True)
    le2 = jnp.where(lane == i1, -jnp.inf, le)
    v2 = jnp.max(le2, axis=-1, keepdims=True)
    i2 = jnp.min(jnp.where(le2 == v2, lane, big), axis=-1, keepdims=True)
    e2 = jnp.exp(v2 - v1)
    tot = 1.0 + e2
    comb = jnp.where(lane == i1, p_g * (1.0 / tot), 0.0) + jnp.where(lane == i2, p_g * (e2 / tot), 0.0)
    return comb, g_sel


def _moe_kernel(x_ref, rw_ref, rb_ref, wg_ref, wu_ref, wd_ref, g_ref, beta_ref, o_ref,
                xs_ref, combs_ref, acc_ref, pos_ref, tab_ref):
    step = pl.program_id(1)
    tm = x_ref.shape[0]
    n_sorted = xs_ref.shape[0]
    blk = MOE_BLOCK

    @pl.when(step == 0)
    def _():
        xb = x_ref[...].astype(BF16)
        comb, g_sel = _route(_dot(xb, rw_ref[...]) + rb_ref[...])
        lane = lax.broadcasted_iota(jnp.int32, (tm, LANES), 1)
        onehot = jnp.where(lane == g_sel, 1.0, 0.0)
        onehot_bf = onehot.astype(BF16)
        col = lax.broadcasted_iota(jnp.int32, (blk, tm), 1)
        row = lax.broadcasted_iota(jnp.int32, (blk, tm), 0)
        cum = jnp.concatenate(
            [_dot(jnp.where(col <= row + r, 1.0, 0.0).astype(BF16), onehot_bf) for r in range(0, tm, blk)], axis=0)
        count = cum[tm - 1:tm, :]
        padded = jnp.floor((count + (blk - 1)) * (1.0 / blk)) * blk
        lane1 = lax.broadcasted_iota(jnp.int32, (1, LANES), 1)
        offsets = jnp.zeros((1, LANES), F32)
        for grp in range(N_GROUPS):
            off = jnp.sum(jnp.where(lane1 < grp, padded, 0.0), axis=-1, keepdims=True)
            n_blocks = jnp.sum(jnp.where(lane1 == grp, padded, 0.0), axis=-1, keepdims=True) * (1.0 / blk)
            offsets = jnp.where(lane1 == grp, off, offsets)
            tab_ref[grp] = off.astype(jnp.int32)[0, 0]
            tab_ref[N_GROUPS + grp] = n_blocks.astype(jnp.int32)[0, 0]
        pos = jnp.sum(onehot * (offsets + cum), axis=-1, keepdims=True) - 1.0
        pos_ref[...] = pos.astype(jnp.int32)
        pos_hi = jnp.floor(pos * (1.0 / 256.0))
        digits = jnp.where(lane == 0, pos_hi, jnp.where(lane == 1, pos - 256.0 * pos_hi, 0.0)).astype(BF16)
        eye = jnp.where(lax.broadcasted_iota(jnp.int32, (16, LANES), 0)
                        == lax.broadcasted_iota(jnp.int32, (16, LANES), 1), 1.0, 0.0).astype(BF16)
        dig_rows = _dot_nt(eye, digits)
        pos_row = (dig_rows[0:1, :] * 256.0 + dig_rows[1:2, :]).astype(jnp.int32)
        c_hi = comb.astype(BF16)
        c_lo = (comb - c_hi.astype(F32)).astype(BF16)
        payload = jnp.concatenate([xb, c_hi, c_lo], axis=1)
        for r in range(0, n_sorted, blk):
            perm = jnp.where(row == pos_row - r, 1.0, 0.0).astype(BF16)
            moved = _dot(perm, payload)
            xs_ref[r:r + blk, :] = moved[:, :D_MODEL].astype(BF16)
            combs_ref[r:r + blk, :] = moved[:, D_MODEL:D_MODEL + LANES] + moved[:, D_MODEL + LANES:]
        acc_ref[...] = jnp.zeros_like(acc_ref)

    grp = (step * MOE_EXPERTS_PER_STEP) // E_PER_GROUP
    start = tab_ref[grp]
    lane_b = lax.broadcasted_iota(jnp.int32, (blk, LANES), 1)

    def expert_block(b, carry):
        r0 = pl.multiple_of(start + b * blk, blk)
        xsb = xs_ref[pl.ds(r0, blk), :]
        combs = combs_ref[pl.ds(r0, blk), :]
        acc = acc_ref[pl.ds(r0, blk), :]
        for j in range(MOE_EXPERTS_PER_STEP):
            gate = _dot(xsb, wg_ref[j])
            h = gate * jax.nn.sigmoid(gate) * _dot(xsb, wu_ref[j])
            y = _dot(h.astype(BF16), wd_ref[j])
            lane_e = N_GROUPS + step * MOE_EXPERTS_PER_STEP + j
            acc = acc + jnp.sum(jnp.where(lane_b == lane_e, combs, 0.0), axis=-1, keepdims=True) * y
        acc_ref[pl.ds(r0, blk), :] = acc
        return carry

    lax.fori_loop(0, tab_ref[N_GROUPS + grp], expert_block, 0)

    @pl.when(step == N_EXPERTS // MOE_EXPERTS_PER_STEP - 1)
    def _():
        for r in range(0, n_sorted, blk):
            xs_ref[r:r + blk, :] = acc_ref[r:r + blk, :].astype(BF16)
        lane_s = lax.broadcasted_iota(jnp.int32, (blk, n_sorted), 1)
        for r in range(0, tm, blk):
            back = jnp.where(lane_s == pos_ref[r:r + blk, :], 1.0, 0.0).astype(BF16)
            y = _dot(back, xs_ref[...])
            o_ref[r:r + blk, :] = _layer_norm(ALPHA * x_ref[r:r + blk, :] + y, g_ref[...], beta_ref[...])


def _moe(x2d, rw_bf16, rb, wg, wu, wd, g, beta, tm):
    n = x2d.shape[0]
    n_sorted = tm + N_GROUPS * MOE_BLOCK
    per_step = MOE_EXPERTS_PER_STEP
    assert tm % MOE_BLOCK == 0 and n_sorted < 256 * 256 and E_PER_GROUP % per_step == 0
    const = lambda shape: pl.BlockSpec(shape, lambda i, e: (0, 0))
    return pl.pallas_call(
        _moe_kernel,
        grid=(n // tm, N_EXPERTS // per_step),
        in_specs=[pl.BlockSpec((tm, D_MODEL), lambda i, e: (i, 0)),
                  const((D_MODEL, LANES)), const((1, LANES)),
                  pl.BlockSpec((per_step, D_MODEL, D_EXPERT), lambda i, e: (e, 0, 0)),
                  pl.BlockSpec((per_step, D_MODEL, D_EXPERT), lambda i, e: (e, 0, 0)),
                  pl.BlockSpec((per_step, D_EXPERT, D_MODEL), lambda i, e: (e, 0, 0)),
                  const((1, D_MODEL)), const((1, D_MODEL))],
        out_specs=pl.BlockSpec((tm, D_MODEL), lambda i, e: (i, 0)),
        out_shape=jax.ShapeDtypeStruct((n, D_MODEL), F32),
        scratch_shapes=[pltpu.VMEM((n_sorted, D_MODEL), BF16),
                        pltpu.VMEM((n_sorted, LANES), F32), pltpu.VMEM((n_sorted, D_MODEL), F32),
                        pltpu.VMEM((tm, 1), jnp.int32), pltpu.SMEM((2 * N_GROUPS,), jnp.int32)],
        compiler_params=_cparams("parallel", "arbitrary"),
        name="moe_ln",
    )(x2d, rw_bf16, rb, wg, wu, wd, g, beta)


def _band_bias(rel_table):
    n_clipped = BAND_W - 1 - REL_MAX
    ext = jnp.concatenate([rel_table[:, REL_MAX - (CHUNK - 1):],
                           jnp.broadcast_to(rel_table[:, -1:], (HB, n_clipped))], axis=1)
    rev = ext[:, ::-1]
    return jnp.stack([rev[:, CHUNK - 1 - i:CHUNK - 1 - i + BAND_W] for i in range(CHUNK)], axis=1).astype(F32)


def _band_bias_grouped(bias):
    g_n = BAND_GROUP
    tabs = [jnp.pad(bias, ((0, 0), (0, 0), (g * CHUNK, (g_n - 1 - g) * CHUNK)), constant_values=-jnp.inf)
            for g in range(g_n)]
    return jnp.stack(tabs, axis=1).reshape(HB * g_n * CHUNK, (g_n + BAND_CHUNKS) * CHUNK)


def _layer_weights(l, w_in, b_if, conv_w, conv_b, rel_table, c_norm_g, w_out, ln1_g, ln1_b,
                   router_g_w, router_g_b, router_e_w, router_e_b, exp_w_gate, exp_w_up, exp_w_down,
                   ln2_g, ln2_b):
    pad_cols = lambda a: jnp.pad(a, ((0, 0), (0, LANES - a.shape[1])))
    bias = _band_bias(rel_table[l])
    return dict(
        w_in=jnp.concatenate([w_in[l, :, :Z_MAIN], pad_cols(w_in[l, :, Z_MAIN:])], axis=1).astype(BF16),
        bif=pad_cols(b_if[l][None, :]),
        conv_w=conv_w[l], conv_b=conv_b[l][None, :],
        bias=bias.reshape(HB * CHUNK, BAND_W), bias_grouped=_band_bias_grouped(bias),
        norm_g=c_norm_g[l].reshape(1, DC),
        w_out=w_out[l].astype(BF16),
        ln1_g=ln1_g[l][None, :], ln1_b=ln1_b[l][None, :],
        rw=pad_cols(jnp.concatenate([router_g_w[l], router_e_w[l]], axis=1)).astype(BF16),
        rb=pad_cols(jnp.concatenate([router_g_b[l], router_e_b[l]])[None, :]),
        wg=exp_w_gate[l].astype(BF16), wu=exp_w_up[l].astype(BF16), wd=exp_w_down[l].astype(BF16),
        ln2_g=ln2_g[l][None, :], ln2_b=ln2_b[l][None, :],
    )


def _layer(x, w, cache, *, tm, tm_moe, sb_bq, length, band_keep):
    batch, t, _ = x.shape
    n = batch * t
    tm, tm_moe = min(tm, n), min(tm_moe, n)
    x2d = x.reshape(n, D_MODEL)
    qa, ka, va, ka_bf, va_bf, qb, kb, vb, kb_bf, vb_bf, u, vc_bf, oc, gates = _in_proj(x2d, w["w_in"], tm)
    per_batch = lambda a: a.reshape(batch, t, a.shape[-1])

    if cache is None:
        past = 0
        sb_k, sb_v = per_batch(ka_bf), per_batch(va_bf)
        gq = BAND_GROUP * CHUNK
        assert t % gq == 0 and BAND_CHUNKS * CHUNK % gq == 0
        band = dict(k=kb_bf, v=vb_bf, bias=w["bias_grouped"], steps=t // gq, gq=gq,
                    n_kb=BAND_CHUNKS * CHUNK // gq + 1, kbs=gq)
        cn0 = jnp.zeros((batch, HC, HDC, 2 * HDC), F32)
        m0 = jnp.zeros((batch, HC, 8, LANES), F32)
        conv0 = jnp.zeros((batch, 8, 2 * DC), F32)
    else:
        a_k, a_v, b_k, b_v, c0, n0, m_init, conv_init = cache
        past = a_k.shape[1]
        tk_pad = -(past + t) % SB_BLOCK
        sb_cat = lambda old, new: jnp.pad(
            jnp.concatenate([old.reshape(batch, past, DA).astype(BF16), per_batch(new)], axis=1),
            ((0, 0), (0, tk_pad), (0, 0)))
        sb_k, sb_v = sb_cat(a_k, ka_bf), sb_cat(a_v, va_bf)
        hist = b_k.shape[1]
        assert hist == BAND_CHUNKS * CHUNK and t == CHUNK and past % CHUNK == 0 and past >= hist
        band_cat = lambda old, new: jnp.concatenate(
            [old.reshape(batch, hist, DB).astype(BF16), per_batch(new)], axis=1).reshape(batch * BAND_W, DB)
        band = dict(k=band_cat(b_k, kb_bf), v=band_cat(b_v, vb_bf), bias=w["bias"], steps=1, gq=CHUNK,
                    n_kb=1, kbs=BAND_W)
        cn0 = jnp.concatenate([c0, n0[..., None], jnp.zeros((batch, HC, HDC, HDC - 1), F32)], axis=-1)
        m0 = jnp.broadcast_to(m_init[:, :, None, None], (batch, HC, 8, LANES))
        conv0 = jnp.pad(conv_init, ((0, 0), (8 - (CONV_W - 1), 0), (0, 0)))

    a_out = _sb_attention(qa, sb_k, sb_v, batch=batch, tq=t, past=past, bq=sb_bq)
    b_out = _band_attention(qb, band["k"], band["v"], band["bias"], batch=batch, steps=band["steps"],
                            gq=band["gq"], n_kb=band["n_kb"], kbs=band["kbs"])
    c_out, cn1, m1 = _mlstm(u, vc_bf, oc, gates, w["conv_w"], w["conv_b"], w["bif"], w["norm_g"],
                            cn0, m0, conv0, batch=batch, t=t, length=length)
    x1 = _out_proj(x2d, a_out, b_out, c_out, w["w_out"], w["ln1_g"], w["ln1_b"], tm)
    x2 = _moe(x1, w["rw"], w["rb"], w["wg"], w["wu"], w["wd"], w["ln2_g"], w["ln2_b"], tm_moe)
    state = (ka.reshape(batch, t, HA, HDA), va.reshape(batch, t, HA, HDA),
             per_batch(kb)[:, t - band_keep:].reshape(batch, band_keep, HB, HDB),
             per_batch(vb)[:, t - band_keep:].reshape(batch, band_keep, HB, HDB),
             cn1[..., :HDC], cn1[..., HDC], m1[:, :, 0, 0], per_batch(u)[:, t - (CONV_W - 1):])
    return x2.reshape(batch, t, D_MODEL), state


def kernel(x_prompt, x_sample, cache_a_k, cache_a_v, cache_b_k, cache_b_v, state_c_C, state_c_n,
           state_c_m, state_c_conv, w_in, b_if, conv_w, conv_b, rel_table, c_norm_g, w_out, ln1_g, ln1_b,
           router_g_w, router_g_b, router_e_w, router_e_b, exp_w_gate, exp_w_up, exp_w_down, ln2_g, ln2_b):
    band_keep = cache_b_k.shape[2]
    xp, xs = x_prompt, x_sample
    prompt_states, sample_states = [], []
    for l in range(DEPTH):
        w = _layer_weights(l, w_in, b_if, conv_w, conv_b, rel_table, c_norm_g, w_out, ln1_g, ln1_b,
                           router_g_w, router_g_b, router_e_w, router_e_b, exp_w_gate, exp_w_up,
                           exp_w_down, ln2_g, ln2_b)
        xp, sp = _layer(xp, w, None, tm=512, tm_moe=1024, sb_bq=SB_BLOCK, length=2 * CHUNK,
                        band_keep=min(band_keep, xp.shape[1]))
        prompt_states.append(sp)
        cache_l = (cache_a_k[l], cache_a_v[l], cache_b_k[l], cache_b_v[l],
                   state_c_C[l], state_c_n[l], state_c_m[l], state_c_conv[l])
        xs, ss = _layer(xs, w, cache_l, tm=512, tm_moe=512, sb_bq=CHUNK, length=CHUNK,
                        band_keep=xs.shape[1])
        sample_states.append(ss)
    p = [jnp.stack(s) for s in zip(*prompt_states)]
    s = [jnp.stack(s) for s in zip(*sample_states)]
    return (xp, xs, *p, *s)
```

```python
import functools

import jax
import jax.numpy as jnp
from jax import lax
from jax.experimental import pallas as pl
from jax.experimental.pallas import tpu as pltpu

F32 = jnp.float32
BF16 = jnp.bfloat16

D_MODEL = 1024
DEPTH = 2
CHUNK = 64
HA, HDA = 4, 64
HB, HDB = 4, 64
HC, HDC = 4, 128
DA, DB, DC = HA * HDA, HB * HDB, HC * HDC
BAND_CHUNKS = 8
BAND_W = (BAND_CHUNKS + 1) * CHUNK
REL_MAX = 128
CONV_W = 4
N_GROUPS = 4
E_PER_GROUP = 4
N_EXPERTS = N_GROUPS * E_PER_GROUP
D_EXPERT = 512
ALPHA = (2 * DEPTH) ** 0.25
LN_EPS = 1e-5

LANES = 128
Z_MAIN = 3 * DA + 3 * DB + 4 * DC
Z_COLS = Z_MAIN + LANES
SB_BLOCK = 128
SB_STAY_CUTOFF = 104.0
BAND_GROUP = 4
MOE_BLOCK = 128
MOE_EXPERTS_PER_STEP = 2
VMEM_LIMIT = 48 * 1024 * 1024


def _cparams(*sem):
    return pltpu.CompilerParams(dimension_semantics=sem, vmem_limit_bytes=VMEM_LIMIT)


def _split3(x):
    hi = x.astype(BF16)
    r1 = x - hi.astype(F32)
    mid = r1.astype(BF16)
    lo = (r1 - mid.astype(F32)).astype(BF16)
    return hi, mid, lo


def _dot(a, b):
    return jnp.dot(a, b, preferred_element_type=F32)


def _dot_nt(a, b):
    return lax.dot_general(a, b, (((1,), (1,)), ((), ())), preferred_element_type=F32)


def _dot_tn(a, b):
    return lax.dot_general(a, b, (((0,), (0,)), ((), ())), preferred_element_type=F32)


def _log_sigmoid_pair(z):
    t = jnp.log1p(jnp.exp(-jnp.abs(z)))
    return -(jnp.maximum(-z, 0.0) + t), -(jnp.maximum(z, 0.0) + t)


def _stack_heads(q, n_heads, width):
    lane = lax.broadcasted_iota(jnp.int32, q.shape, 1)
    zero = jnp.zeros_like(q)
    return jnp.concatenate(
        [jnp.where((lane >= h * width) & (lane < (h + 1) * width), q, zero) for h in range(n_heads)], axis=0)


def _unstack_heads(acc, n_heads, width):
    rows = acc.shape[0] // n_heads
    lane = lax.broadcasted_iota(jnp.int32, (rows, n_heads * width), 1)
    out = acc[0:rows]
    for h in range(1, n_heads):
        out = jnp.where(lane >= h * width, acc[h * rows:(h + 1) * rows], out)
    return out


def _in_proj_kernel(x_ref, w_ref, qa, ka, va, kab, vab, qb, kb, vb, kbb, vbb, u, vcb, oc, g):
    xb = x_ref[...].astype(BF16)

    def proj(lo, width):
        return _dot(xb, w_ref[:, lo:lo + width])

    qa[...] = (proj(0, DA) * (HDA ** -0.5)).astype(BF16)
    for f32_ref, bf_ref, lo in ((ka, kab, DA), (va, vab, 2 * DA), (kb, kbb, 3 * DA + DB), (vb, vbb, 3 * DA + 2 * DB)):
        t = proj(lo, DA)
        f32_ref[...] = t
        bf_ref[...] = t.astype(BF16)
    qb[...] = (proj(3 * DA, DB) * (HDB ** -0.5)).astype(BF16)
    base = 3 * DA + 3 * DB
    for c in range(2):
        u[:, c * DC:(c + 1) * DC] = proj(base + c * DC, DC)
    vcb[...] = proj(base + 2 * DC, DC).astype(BF16)
    oc[...] = proj(base + 3 * DC, DC)
    g[...] = proj(Z_MAIN, LANES)


def _in_proj(x2d, w_bf16, tm):
    n = x2d.shape[0]
    widths_dtypes = [(DA, BF16), (DA, F32), (DA, F32), (DA, BF16), (DA, BF16),
                     (DB, BF16), (DB, F32), (DB, F32), (DB, BF16), (DB, BF16),
                     (2 * DC, F32), (DC, BF16), (DC, F32), (LANES, F32)]
    return pl.pallas_call(
        _in_proj_kernel,
        grid=(n // tm,),
        in_specs=[pl.BlockSpec((tm, D_MODEL), lambda i: (i, 0)),
                  pl.BlockSpec((D_MODEL, Z_COLS), lambda i: (0, 0))],
        out_specs=[pl.BlockSpec((tm, w), lambda i: (i, 0)) for w, _ in widths_dtypes],
        out_shape=[jax.ShapeDtypeStruct((n, w), dt) for w, dt in widths_dtypes],
        compiler_params=_cparams("parallel"),
        name="in_proj",
    )(x2d, w_bf16)


def _sb_kernel(q_ref, k_ref, v_ref, o_ref, carry_ref, acc_ref, *, past, bq, bk, n_masked):
    i = pl.program_id(1)
    qs = _stack_heads(q_ref[...], HA, HDA)
    jj = lax.broadcasted_iota(jnp.int32, (bk, 2 * bk), 0)
    ss = lax.broadcasted_iota(jnp.int32, (bk, 2 * bk), 1)
    csum = jnp.where((ss >= bk) | (jj > ss), 1.0, 0.0).astype(BF16)
    csum2 = jnp.concatenate([csum, csum], axis=0)
    kb_last = (past + (i + 1) * bq - 2) // bk

    def block(kb, masked, first):
        start = pl.multiple_of(kb * bk, bk)
        s = _dot_nt(qs, k_ref[pl.ds(start, bk), :])
        cost = jnp.maximum(s, 0.0) + jnp.log(1.0 + jnp.exp(-jnp.abs(s)))
        if masked:
            row = lax.broadcasted_iota(jnp.int32, (HA * bq, bk), 0) & (bq - 1)
            mask = (start + lax.broadcasted_iota(jnp.int32, (HA * bq, bk), 1)) < past + i * bq + row
            stay = jnp.where(mask, cost, 0.0)
        else:
            stay = cost
        hi = stay.astype(BF16)
        lo = (stay - hi.astype(F32)).astype(BF16)
        cs = _dot(jnp.concatenate([hi, lo], axis=1), csum2)
        before = cs[:, :bk] if first else carry_ref[...] + cs[:, :bk]
        w = jnp.exp(s - cost - before)
        if masked:
            w = jnp.where(mask, w, 0.0)
        pv = _dot(w.astype(BF16), v_ref[pl.ds(start, bk), :])
        if first:
            carry = cs[:, bk:]
            acc_ref[...] = pv
        else:
            carry = carry_ref[...] + cs[:, bk:]
            acc_ref[...] += pv
        carry_ref[...] = carry
        return (jnp.min(carry) > SB_STAY_CUTOFF).astype(jnp.int32)

    done0 = block(kb_last, True, True)
    for d in range(1, n_masked):
        done0 = block(kb_last - d, True, False)

    def cond(state):
        kb, done = state
        return jnp.logical_and(kb >= 0, done == 0)

    def body(state):
        kb, _ = state
        return kb - 1, block(kb, False, False)

    lax.while_loop(cond, body, (kb_last - n_masked, done0))
    o_ref[...] = _unstack_heads(acc_ref[...], HA, HDA).astype(o_ref.dtype)


def _sb_attention(q_bf16, k_bf16, v_bf16, *, batch, tq, past, bq):
    bk = SB_BLOCK
    nq = tq // bq
    tk = k_bf16.shape[1]
    assert tk % bk == 0 and (past + tq - 2) // bk < tk // bk
    assert past % bk == 0 and (bq % bk == 0 or (nq == 1 and bq <= bk))
    n_masked = max(bq // bk, 1)
    return pl.pallas_call(
        functools.partial(_sb_kernel, past=past, bq=bq, bk=bk, n_masked=n_masked),
        grid=(batch, nq),
        in_specs=[pl.BlockSpec((bq, DA), lambda b, i: (b * nq + i, 0)),
                  pl.BlockSpec((None, tk, DA), lambda b, i: (b, 0, 0)),
                  pl.BlockSpec((None, tk, DA), lambda b, i: (b, 0, 0))],
        out_specs=pl.BlockSpec((bq, DA), lambda b, i: (b * nq + i, 0)),
        out_shape=jax.ShapeDtypeStruct((batch * tq, DA), BF16),
        scratch_shapes=[pltpu.VMEM((HA * bq, bk), F32), pltpu.VMEM((HA * bq, DA), F32)],
        compiler_params=_cparams("parallel", "arbitrary"),
        name="sb_attention",
    )(q_bf16, k_bf16, v_bf16)


def _band_kernel(q_ref, *refs, n_kb, rows_per_pass):
    k_refs, v_refs = refs[:n_kb], refs[n_kb:2 * n_kb]
    bias_ref, o_ref, s_ref, p_ref = refs[2 * n_kb:]
    j = pl.program_id(1)
    qs = _stack_heads(q_ref[...], HB, HDB)
    kbs = k_refs[0].shape[0]
    for i in range(n_kb):
        s = _dot_nt(qs, k_refs[i][...]) + bias_ref[:, i * kbs:(i + 1) * kbs]
        if i < n_kb - 1:
            s = s + jnp.where(j + i < n_kb - 1, -jnp.inf, 0.0)
        s_ref[:, i * kbs:(i + 1) * kbs] = s
    for r in range(0, s_ref.shape[0], rows_per_pass):
        s = s_ref[r:r + rows_per_pass, :]
        e = jnp.exp(s - jnp.max(s, axis=-1, keepdims=True))
        p_ref[r:r + rows_per_pass, :] = (e * (1.0 / jnp.sum(e, axis=-1, keepdims=True))).astype(BF16)
    acc = _dot(p_ref[:, 0:kbs], v_refs[0][...])
    for i in range(1, n_kb):
        acc = acc + _dot(p_ref[:, i * kbs:(i + 1) * kbs], v_refs[i][...])
    o_ref[...] = _unstack_heads(acc, HB, HDB).astype(o_ref.dtype)


def _band_attention(q_bf16, k_bf16, v_bf16, bias, *, batch, steps, gq, n_kb, kbs):
    blocks_per_batch = k_bf16.shape[0] // (batch * kbs)
    kv_specs = [pl.BlockSpec((kbs, DB), functools.partial(
        lambda b, j, i: (b * blocks_per_batch + jnp.maximum(j + i - (n_kb - 1), 0), 0), i=i)) for i in range(n_kb)]
    return pl.pallas_call(
        functools.partial(_band_kernel, n_kb=n_kb, rows_per_pass=min(128, HB * gq)),
        grid=(batch, steps),
        in_specs=[pl.BlockSpec((gq, DB), lambda b, j: (b * steps + j, 0))] + kv_specs + kv_specs
        + [pl.BlockSpec((HB * gq, n_kb * kbs), lambda b, j: (0, 0))],
        out_specs=pl.BlockSpec((gq, DB), lambda b, j: (b * steps + j, 0)),
        out_shape=jax.ShapeDtypeStruct((batch * steps * gq, DB), BF16),
        scratch_shapes=[pltpu.VMEM((HB * gq, n_kb * kbs), F32), pltpu.VMEM((HB * gq, n_kb * kbs), BF16)],
        compiler_params=_cparams("parallel", "arbitrary"),
        name="band_attention",
    )(q_bf16, *([k_bf16] * n_kb), *([v_bf16] * n_kb), bias)


def _mlstm_kernel(u_ref, vc_ref, oc_ref, g_ref, convw_ref, convb_ref, bif_ref, ng_ref,
                  cn0_ref, m0_ref, conv0_ref, o_ref, cn_ref, m_ref, ubuf, *, length):
    L = length
    c = pl.program_id(1)

    @pl.when(c == 0)
    def _():
        cn_ref[...] = cn0_ref[...]
        m_ref[...] = m0_ref[...]
        ubuf[0:8, :] = conv0_ref[...]

    ubuf[8:8 + L, :] = u_ref[...]
    y = convb_ref[...]
    for w in range(CONV_W):
        y = y + ubuf[8 - (CONV_W - 1) + w:8 - (CONV_W - 1) + w + L, :] * convw_ref[w:w + 1, :]
    tail = ubuf[L:L + 8, :]
    ubuf[0:8, :] = tail
    qk = y * jax.nn.sigmoid(y)

    g = g_ref[...] + bif_ref[...]
    lf_all, _ = _log_sigmoid_pair(g)
    r_i = lax.broadcasted_iota(jnp.int32, (L, L), 0)
    c_i = lax.broadcasted_iota(jnp.int32, (L, L), 1)
    causal = c_i <= r_i
    tril = jnp.where(causal, 1.0, 0.0).astype(BF16)
    hi, mid, lo = _split3(lf_all)
    b_all = _dot(tril, hi) + _dot(tril, mid) + _dot(tril, lo)
    lane = lax.broadcasted_iota(jnp.int32, (L, LANES), 1)
    mixed = jnp.where(lane < HC, g, b_all)
    eye = jnp.where(lax.broadcasted_iota(jnp.int32, (16, LANES), 0)
                    == lax.broadcasted_iota(jnp.int32, (16, LANES), 1), 1.0, 0.0).astype(BF16)
    hi, mid, lo = _split3(mixed)
    rows = _dot_nt(eye, hi) + _dot_nt(eye, mid) + _dot_nt(eye, lo)
    ones_col = jnp.where(lax.broadcasted_iota(jnp.int32, (L, HDC), 1) == 0, 1.0, 0.0).astype(BF16)

    for h in range(HC):
        li_row = rows[h:h + 1, :]
        b_row = rows[HC + h:HC + h + 1, :]
        li_col = g[:, h:h + 1]
        b_col = b_all[:, HC + h:HC + h + 1]
        dmat = jnp.where(causal, b_col - b_row + li_row, -jnp.inf)
        m_prev = m_ref[h][0:1, 0:1]
        inter = b_col + m_prev
        m_t = jnp.maximum(inter, jnp.max(dmat, axis=-1, keepdims=True))
        w_intra = jnp.exp(dmat - m_t)
        w_inter = jnp.exp(inter - m_t)
        qh = qk[:, h * HDC:(h + 1) * HDC].astype(BF16)
        kf = qk[:, DC + h * HDC:DC + (h + 1) * HDC] * (HDC ** -0.5)
        vaug = jnp.concatenate([vc_ref[:, h * HDC:(h + 1) * HDC], ones_col], axis=1)
        s = _dot_nt(qh, kf.astype(BF16)) * w_intra
        cn = cn_ref[h]
        qcn = _dot(qh, cn.astype(BF16))
        sv = _dot(s.astype(BF16), vaug[:, :HDC])
        num = w_inter * qcn[:, :HDC] + sv
        den = w_inter * qcn[:, HDC:HDC + 1] + jnp.sum(s, axis=-1, keepdims=True)
        hh = num / jnp.maximum(jnp.abs(den), jnp.exp(-m_t))
        m_new = m_t[L - 1:L, :]
        decay = jnp.exp(inter[L - 1:L, :] - m_new)
        w_k = jnp.exp(b_col[L - 1:L, :] - b_col + li_col - m_new)
        cn_ref[h] = decay * cn + _dot_tn((kf * w_k).astype(BF16), vaug)
        m_ref[h] = jnp.broadcast_to(m_new, (8, LANES))
        mu = jnp.mean(hh, axis=-1, keepdims=True)
        dev = hh - mu
        var = jnp.mean(dev * dev, axis=-1, keepdims=True)
        hn = dev * lax.rsqrt(var + LN_EPS) * ng_ref[:, h * HDC:(h + 1) * HDC]
        gate = jax.nn.sigmoid(oc_ref[:, h * HDC:(h + 1) * HDC])
        o_ref[:, h * HDC:(h + 1) * HDC] = (gate * hn).astype(o_ref.dtype)


def _mlstm(u, vc_bf16, oc, gates, conv_w, conv_b, bif_pad, norm_g, cn0, m0, conv0, *, batch, t, length):
    nc = t // length
    row = lambda w: pl.BlockSpec((length, w), lambda b, c: (b * nc + c, 0))
    const = lambda shape: pl.BlockSpec(shape, lambda b, c: (0,) * len(shape))
    per_batch = lambda shape: pl.BlockSpec((None,) + shape, lambda b, c: (b,) + (0,) * len(shape))
    return pl.pallas_call(
        functools.partial(_mlstm_kernel, length=length),
        grid=(batch, nc),
        in_specs=[row(2 * DC), row(DC), row(DC), row(LANES),
                  const((CONV_W, 2 * DC)), const((1, 2 * DC)), const((1, LANES)), const((1, DC)),
                  per_batch((HC, HDC, 2 * HDC)), per_batch((HC, 8, LANES)), per_batch((8, 2 * DC))],
        out_specs=[row(DC), per_batch((HC, HDC, 2 * HDC)), per_batch((HC, 8, LANES))],
        out_shape=[jax.ShapeDtypeStruct((batch * t, DC), BF16),
                   jax.ShapeDtypeStruct((batch, HC, HDC, 2 * HDC), F32),
                   jax.ShapeDtypeStruct((batch, HC, 8, LANES), F32)],
        scratch_shapes=[pltpu.VMEM((8 + length, 2 * DC), F32)],
        compiler_params=_cparams("parallel", "arbitrary"),
        name="mlstm",
    )(u, vc_bf16, oc, gates, conv_w, conv_b, bif_pad, norm_g, cn0, m0, conv0)


def _layer_norm(y, g, b):
    mu = jnp.mean(y, axis=-1, keepdims=True)
    dev = y - mu
    var = jnp.mean(dev * dev, axis=-1, keepdims=True)
    return dev * lax.rsqrt(var + LN_EPS) * g + b


def _out_proj_kernel(x_ref, a_ref, b_ref, c_ref, w_ref, g_ref, beta_ref, o_ref):
    mix = (_dot(a_ref[...], w_ref[0:DA, :]) + _dot(b_ref[...], w_ref[DA:DA + DB, :])
           + _dot(c_ref[...], w_ref[DA + DB:, :]))
    o_ref[...] = _layer_norm(ALPHA * x_ref[...] + mix, g_ref[...], beta_ref[...])


def _out_proj(x2d, a, b, c, w_bf16, g, beta, tm):
    n = x2d.shape[0]
    row = lambda w: pl.BlockSpec((tm, w), lambda i: (i, 0))
    const = lambda shape: pl.BlockSpec(shape, lambda i: (0, 0))
    return pl.pallas_call(
        _out_proj_kernel,
        grid=(n // tm,),
        in_specs=[row(D_MODEL), row(DA), row(DB), row(DC), const((D_MODEL, D_MODEL)),
                  const((1, D_MODEL)), const((1, D_MODEL))],
        out_specs=row(D_MODEL),
        out_shape=jax.ShapeDtypeStruct((n, D_MODEL), F32),
        compiler_params=_cparams("parallel"),
        name="out_proj_ln",
    )(x2d, a, b, c, w_bf16, g, beta)


def _route(logits):
    lane = lax.broadcasted_iota(jnp.int32, logits.shape, 1)
    big = jnp.int32(LANES)
    is_g = lane < N_GROUPS
    lg = jnp.where(is_g, logits, -jnp.inf)
    g_max = jnp.max(lg, axis=-1, keepdims=True)
    g_sel = jnp.min(jnp.where(lg == g_max, lane, big), axis=-1, keepdims=True)
    p_g = 1.0 / jnp.sum(jnp.where(is_g, jnp.exp(logits - g_max), 0.0), axis=-1, keepdims=True)
    e_lo = N_GROUPS + g_sel * E_PER_GROUP
    in_group = (lane >= e_lo) & (lane < e_lo + E_PER_GROUP)
    le = jnp.where(in_group, logits, -jnp.inf)
    v1 = jnp.max(le, axis=-1, keepdims=True)
    i1 = jnp.min(jnp.where(le == v1, lane, big), axis=-1, keepdims=True)
    le2 = jnp.where(lane == i1, -jnp.inf, le)
    v2 = jnp.max(le2, axis=-1, keepdims=True)
    i2 = jnp.min(jnp.where(le2 == v2, lane, big), axis=-1, keepdims=True)
    e2 = jnp.exp(v2 - v1)
    tot = 1.0 + e2
    comb = jnp.where(lane == i1, p_g * (1.0 / tot), 0.0) + jnp.where(lane == i2, p_g * (e2 / tot), 0.0)
    return comb, g_sel


def _moe_kernel(x_ref, rw_ref, rb_ref, wg_ref, wu_ref, wd_ref, g_ref, beta_ref, o_ref,
                xs_ref, combs_ref, acc_ref, pos_ref, tab_ref):
    step = pl.program_id(1)
    tm = x_ref.shape[0]
    n_sorted = xs_ref.shape[0]
    blk = MOE_BLOCK

    @pl.when(step == 0)
    def _():
        xb = x_ref[...].astype(BF16)
        comb, g_sel = _route(_dot(xb, rw_ref[...]) + rb_ref[...])
        lane = lax.broadcasted_iota(jnp.int32, (tm, LANES), 1)
        onehot = jnp.where(lane == g_sel, 1.0, 0.0)
        onehot_bf = onehot.astype(BF16)
        col = lax.broadcasted_iota(jnp.int32, (blk, tm), 1)
        row = lax.broadcasted_iota(jnp.int32, (blk, tm), 0)
        cum = jnp.concatenate(
            [_dot(jnp.where(col <= row + r, 1.0, 0.0).astype(BF16), onehot_bf) for r in range(0, tm, blk)], axis=0)
        count = cum[tm - 1:tm, :]
        padded = jnp.floor((count + (blk - 1)) * (1.0 / blk)) * blk
        lane1 = lax.broadcasted_iota(jnp.int32, (1, LANES), 1)
        offsets = jnp.zeros((1, LANES), F32)
        for grp in range(N_GROUPS):
            off = jnp.sum(jnp.where(lane1 < grp, padded, 0.0), axis=-1, keepdims=True)
            n_blocks = jnp.sum(jnp.where(lane1 == grp, padded, 0.0), axis=-1, keepdims=True) * (1.0 / blk)
            offsets = jnp.where(lane1 == grp, off, offsets)
            tab_ref[grp] = off.astype(jnp.int32)[0, 0]
            tab_ref[N_GROUPS + grp] = n_blocks.astype(jnp.int32)[0, 0]
        pos = jnp.sum(onehot * (offsets + cum), axis=-1, keepdims=True) - 1.0
        pos_ref[...] = pos.astype(jnp.int32)
        pos_hi = jnp.floor(pos * (1.0 / 256.0))
        digits = jnp.where(lane == 0, pos_hi, jnp.where(lane == 1, pos - 256.0 * pos_hi, 0.0)).astype(BF16)
        eye = jnp.where(lax.broadcasted_iota(jnp.int32, (16, LANES), 0)
                        == lax.broadcasted_iota(jnp.int32, (16, LANES), 1), 1.0, 0.0).astype(BF16)
        dig_rows = _dot_nt(eye, digits)
        pos_row = (dig_rows[0:1, :] * 256.0 + dig_rows[1:2, :]).astype(jnp.int32)
        c_hi = comb.astype(BF16)
        c_lo = (comb - c_hi.astype(F32)).astype(BF16)
        payload = jnp.concatenate([xb, c_hi, c_lo], axis=1)
        for r in range(0, n_sorted, blk):
            perm = jnp.where(row == pos_row - r, 1.0, 0.0).astype(BF16)
            moved = _dot(perm, payload)
            xs_ref[r:r + blk, :] = moved[:, :D_MODEL].astype(BF16)
            combs_ref[r:r + blk, :] = moved[:, D_MODEL:D_MODEL + LANES] + moved[:, D_MODEL + LANES:]
        acc_ref[...] = jnp.zeros_like(acc_ref)

    grp = (step * MOE_EXPERTS_PER_STEP) // E_PER_GROUP
    start = tab_ref[grp]
    n_blocks = tab_ref[N_GROUPS + grp]

    def expert_rows(r0, rows):
        xsb = xs_ref[pl.ds(r0, rows), :]
        combs = combs_ref[pl.ds(r0, rows), :]
        acc = acc_ref[pl.ds(r0, rows), :]
        lane_b = lax.broadcasted_iota(jnp.int32, (rows, LANES), 1)
        for j in range(MOE_EXPERTS_PER_STEP):
            gate = _dot(xsb, wg_ref[j])
            h = gate * jax.nn.sigmoid(gate) * _dot(xsb, wu_ref[j])
            y = _dot(h.astype(BF16), wd_ref[j])
            lane_e = N_GROUPS + step * MOE_EXPERTS_PER_STEP + j
            acc = acc + jnp.sum(jnp.where(lane_b == lane_e, combs, 0.0), axis=-1, keepdims=True) * y
        acc_ref[pl.ds(r0, rows), :] = acc

    def block_pair(b, carry):
        expert_rows(pl.multiple_of(start + b * (2 * blk), blk), 2 * blk)
        return carry

    lax.fori_loop(0, n_blocks // 2, block_pair, 0)

    @pl.when(n_blocks % 2 == 1)
    def _():
        expert_rows(pl.multiple_of(start + (n_blocks - 1) * blk, blk), blk)

    @pl.when(step == N_EXPERTS // MOE_EXPERTS_PER_STEP - 1)
    def _():
        for r in range(0, n_sorted, blk):
            xs_ref[r:r + blk, :] = acc_ref[r:r + blk, :].astype(BF16)
        lane_s = lax.broadcasted_iota(jnp.int32, (blk, n_sorted), 1)
        for r in range(0, tm, blk):
            back = jnp.where(lane_s == pos_ref[r:r + blk, :], 1.0, 0.0).astype(BF16)
            y = _dot(back, xs_ref[...])
            o_ref[r:r + blk, :] = _layer_norm(ALPHA * x_ref[r:r + blk, :] + y, g_ref[...], beta_ref[...])


def _moe(x2d, rw_bf16, rb, wg, wu, wd, g, beta, tm):
    n = x2d.shape[0]
    n_sorted = tm + N_GROUPS * MOE_BLOCK
    per_step = MOE_EXPERTS_PER_STEP
    assert tm % MOE_BLOCK == 0 and n_sorted < 256 * 256 and E_PER_GROUP % per_step == 0
    const = lambda shape: pl.BlockSpec(shape, lambda i, e: (0, 0))
    return pl.pallas_call(
        _moe_kernel,
        grid=(n // tm, N_EXPERTS // per_step),
        in_specs=[pl.BlockSpec((tm, D_MODEL), lambda i, e: (i, 0)),
                  const((D_MODEL, LANES)), const((1, LANES)),
                  pl.BlockSpec((per_step, D_MODEL, D_EXPERT), lambda i, e: (e, 0, 0)),
                  pl.BlockSpec((per_step, D_MODEL, D_EXPERT), lambda i, e: (e, 0, 0)),
                  pl.BlockSpec((per_step, D_EXPERT, D_MODEL), lambda i, e: (e, 0, 0)),
                  const((1, D_MODEL)), const((1, D_MODEL))],
        out_specs=pl.BlockSpec((tm, D_MODEL), lambda i, e: (i, 0)),
        out_shape=jax.ShapeDtypeStruct((n, D_MODEL), F32),
        scratch_shapes=[pltpu.VMEM((n_sorted, D_MODEL), BF16),
                        pltpu.VMEM((n_sorted, LANES), F32), pltpu.VMEM((n_sorted, D_MODEL), F32),
                        pltpu.VMEM((tm, 1), jnp.int32), pltpu.SMEM((2 * N_GROUPS,), jnp.int32)],
        compiler_params=_cparams("parallel", "arbitrary"),
        name="moe_ln",
    )(x2d, rw_bf16, rb, wg, wu, wd, g, beta)


def _band_bias(rel_table):
    n_clipped = BAND_W - 1 - REL_MAX
    ext = jnp.concatenate([rel_table[:, REL_MAX - (CHUNK - 1):],
                           jnp.broadcast_to(rel_table[:, -1:], (HB, n_clipped))], axis=1)
    rev = ext[:, ::-1]
    return jnp.stack([rev[:, CHUNK - 1 - i:CHUNK - 1 - i + BAND_W] for i in range(CHUNK)], axis=1).astype(F32)


def _band_bias_grouped(bias):
    g_n = BAND_GROUP
    tabs = [jnp.pad(bias, ((0, 0), (0, 0), (g * CHUNK, (g_n - 1 - g) * CHUNK)), constant_values=-jnp.inf)
            for g in range(g_n)]
    return jnp.stack(tabs, axis=1).reshape(HB * g_n * CHUNK, (g_n + BAND_CHUNKS) * CHUNK)


def _layer_weights(l, w_in, b_if, conv_w, conv_b, rel_table, c_norm_g, w_out, ln1_g, ln1_b,
                   router_g_w, router_g_b, router_e_w, router_e_b, exp_w_gate, exp_w_up, exp_w_down,
                   ln2_g, ln2_b):
    pad_cols = lambda a: jnp.pad(a, ((0, 0), (0, LANES - a.shape[1])))
    bias = _band_bias(rel_table[l])
    return dict(
        w_in=jnp.concatenate([w_in[l, :, :Z_MAIN], pad_cols(w_in[l, :, Z_MAIN:])], axis=1).astype(BF16),
        bif=pad_cols(b_if[l][None, :]),
        conv_w=conv_w[l], conv_b=conv_b[l][None, :],
        bias=bias.reshape(HB * CHUNK, BAND_W), bias_grouped=_band_bias_grouped(bias),
        norm_g=c_norm_g[l].reshape(1, DC),
        w_out=w_out[l].astype(BF16),
        ln1_g=ln1_g[l][None, :], ln1_b=ln1_b[l][None, :],
        rw=pad_cols(jnp.concatenate([router_g_w[l], router_e_w[l]], axis=1)).astype(BF16),
        rb=pad_cols(jnp.concatenate([router_g_b[l], router_e_b[l]])[None, :]),
        wg=exp_w_gate[l].astype(BF16), wu=exp_w_up[l].astype(BF16), wd=exp_w_down[l].astype(BF16),
        ln2_g=ln2_g[l][None, :], ln2_b=ln2_b[l][None, :],
    )


def _layer(x, w, cache, *, tm, tm_moe, sb_bq, length, band_keep):
    batch, t, _ = x.shape
    n = batch * t
    tm, tm_moe = min(tm, n), min(tm_moe, n)
    x2d = x.reshape(n, D_MODEL)
    qa, ka, va, ka_bf, va_bf, qb, kb, vb, kb_bf, vb_bf, u, vc_bf, oc, gates = _in_proj(x2d, w["w_in"], tm)
    per_batch = lambda a: a.reshape(batch, t, a.shape[-1])

    if cache is None:
        past = 0
        sb_k, sb_v = per_batch(ka_bf), per_batch(va_bf)
        gq = BAND_GROUP * CHUNK
        assert t % gq == 0 and BAND_CHUNKS * CHUNK % gq == 0
        band = dict(k=kb_bf, v=vb_bf, bias=w["bias_grouped"], steps=t // gq, gq=gq,
                    n_kb=BAND_CHUNKS * CHUNK // gq + 1, kbs=gq)
        cn0 = jnp.zeros((batch, HC, HDC, 2 * HDC), F32)
        m0 = jnp.zeros((batch, HC, 8, LANES), F32)
        conv0 = jnp.zeros((batch, 8, 2 * DC), F32)
    else:
        a_k, a_v, b_k, b_v, c0, n0, m_init, conv_init = cache
        past = a_k.shape[1]
        tk_pad = -(past + t) % SB_BLOCK
        sb_cat = lambda old, new: jnp.pad(
            jnp.concatenate([old.reshape(batch, past, DA).astype(BF16), per_batch(new)], axis=1),
            ((0, 0), (0, tk_pad), (0, 0)))
        sb_k, sb_v = sb_cat(a_k, ka_bf), sb_cat(a_v, va_bf)
        hist = b_k.shape[1]
        assert hist == BAND_CHUNKS * CHUNK and t == CHUNK and past % CHUNK == 0 and past >= hist
        band_cat = lambda old, new: jnp.concatenate(
            [old.reshape(batch, hist, DB).astype(BF16), per_batch(new)], axis=1).reshape(batch * BAND_W, DB)
        band = dict(k=band_cat(b_k, kb_bf), v=band_cat(b_v, vb_bf), bias=w["bias"], steps=1, gq=CHUNK,
                    n_kb=1, kbs=BAND_W)
        cn0 = jnp.concatenate([c0, n0[..., None], jnp.zeros((batch, HC, HDC, HDC - 1), F32)], axis=-1)
        m0 = jnp.broadcast_to(m_init[:, :, None, None], (batch, HC, 8, LANES))
        conv0 = jnp.pad(conv_init, ((0, 0), (8 - (CONV_W - 1), 0), (0, 0)))

    a_out = _sb_attention(qa, sb_k, sb_v, batch=batch, tq=t, past=past, bq=sb_bq)
    b_out = _band_attention(qb, band["k"], band["v"], band["bias"], batch=batch, steps=band["steps"],
                            gq=band["gq"], n_kb=band["n_kb"], kbs=band["kbs"])
    c_out, cn1, m1 = _mlstm(u, vc_bf, oc, gates, w["conv_w"], w["conv_b"], w["bif"], w["norm_g"],
                            cn0, m0, conv0, batch=batch, t=t, length=length)
    x1 = _out_proj(x2d, a_out, b_out, c_out, w["w_out"], w["ln1_g"], w["ln1_b"], tm)
    x2 = _moe(x1, w["rw"], w["rb"], w["wg"], w["wu"], w["wd"], w["ln2_g"], w["ln2_b"], tm_moe)
    state = (ka.reshape(batch, t, HA, HDA), va.reshape(batch, t, HA, HDA),
             per_batch(kb)[:, t - band_keep:].reshape(batch, band_keep, HB, HDB),
             per_batch(vb)[:, t - band_keep:].reshape(batch, band_keep, HB, HDB),
             cn1[..., :HDC], cn1[..., HDC], m1[:, :, 0, 0], per_batch(u)[:, t - (CONV_W - 1):])
    return x2.reshape(batch, t, D_MODEL), state


def kernel(x_prompt, x_sample, cache_a_k, cache_a_v, cache_b_k, cache_b_v, state_c_C, state_c_n,
           state_c_m, state_c_conv, w_in, b_if, conv_w, conv_b, rel_table, c_norm_g, w_out, ln1_g, ln1_b,
           router_g_w, router_g_b, router_e_w, router_e_b, exp_w_gate, exp_w_up, exp_w_down, ln2_g, ln2_b):
    band_keep = cache_b_k.shape[2]
    xp, xs = x_prompt, x_sample
    prompt_states, sample_states = [], []
    for l in range(DEPTH):
        w = _layer_weights(l, w_in, b_if, conv_w, conv_b, rel_table, c_norm_g, w_out, ln1_g, ln1_b,
                           router_g_w, router_g_b, router_e_w, router_e_b, exp_w_gate, exp_w_up,
                           exp_w_down, ln2_g, ln2_b)
        xp, sp = _layer(xp, w, None, tm=512, tm_moe=1024, sb_bq=2 * SB_BLOCK, length=2 * CHUNK,
                        band_keep=min(band_keep, xp.shape[1]))
        prompt_states.append(sp)
        cache_l = (cache_a_k[l], cache_a_v[l], cache_b_k[l], cache_b_v[l],
                   state_c_C[l], state_c_n[l], state_c_m[l], state_c_conv[l])
        xs, ss = _layer(xs, w, cache_l, tm=512, tm_moe=512, sb_bq=CHUNK, length=CHUNK,
                        band_keep=xs.shape[1])
        sample_states.append(ss)
    p = [jnp.stack(s) for s in zip(*prompt_states)]
    s = [jnp.stack(s) for s in zip(*sample_states)]
    return (xp, xs, *p, *s)
```

```python
import functools

import jax
import jax.numpy as jnp
from jax import lax
from jax.experimental import pallas as pl
from jax.experimental.pallas import tpu as pltpu

F32 = jnp.float32
BF16 = jnp.bfloat16

D_MODEL = 1024
DEPTH = 2
CHUNK = 64
HA, HDA = 4, 64
HB, HDB = 4, 64
HC, HDC = 4, 128
DA, DB, DC = HA * HDA, HB * HDB, HC * HDC
BAND_CHUNKS = 8
BAND_W = (BAND_CHUNKS + 1) * CHUNK
REL_MAX = 128
CONV_W = 4
N_GROUPS = 4
E_PER_GROUP = 4
N_EXPERTS = N_GROUPS * E_PER_GROUP
D_EXPERT = 512
ALPHA = (2 * DEPTH) ** 0.25
LN_EPS = 1e-5

LANES = 128
Z_MAIN = 3 * DA + 3 * DB + 4 * DC
Z_COLS = Z_MAIN + LANES
SB_BLOCK = 128
SB_STAY_CUTOFF = 104.0
BAND_GROUP = 4
MOE_BLOCK = 128
MOE_EXPERTS_PER_STEP = 2
VMEM_LIMIT = 48 * 1024 * 1024


def _cparams(*sem):
    return pltpu.CompilerParams(dimension_semantics=sem, vmem_limit_bytes=VMEM_LIMIT)


def _split3(x):
    hi = x.astype(BF16)
    r1 = x - hi.astype(F32)
    mid = r1.astype(BF16)
    lo = (r1 - mid.astype(F32)).astype(BF16)
    return hi, mid, lo


def _dot(a, b):
    return jnp.dot(a, b, preferred_element_type=F32)


def _dot_nt(a, b):
    return lax.dot_general(a, b, (((1,), (1,)), ((), ())), preferred_element_type=F32)


def _dot_tn(a, b):
    return lax.dot_general(a, b, (((0,), (0,)), ((), ())), preferred_element_type=F32)


def _log_sigmoid_pair(z):
    t = jnp.log1p(jnp.exp(-jnp.abs(z)))
    return -(jnp.maximum(-z, 0.0) + t), -(jnp.maximum(z, 0.0) + t)


def _stack_heads(q, n_heads, width):
    lane = lax.broadcasted_iota(jnp.int32, q.shape, 1)
    zero = jnp.zeros_like(q)
    return jnp.concatenate(
        [jnp.where((lane >= h * width) & (lane < (h + 1) * width), q, zero) for h in range(n_heads)], axis=0)


def _unstack_heads(acc, n_heads, width):
    rows = acc.shape[0] // n_heads
    lane = lax.broadcasted_iota(jnp.int32, (rows, n_heads * width), 1)
    out = acc[0:rows]
    for h in range(1, n_heads):
        out = jnp.where(lane >= h * width, acc[h * rows:(h + 1) * rows], out)
    return out


def _in_proj_kernel(x_ref, w_ref, qa, ka, va, kab, vab, qb, kb, vb, kbb, vbb, u, vcb, oc, g, *, kv_transposed):
    xb = x_ref[...].astype(BF16)

    def proj(lo, width):
        return _dot(xb, w_ref[:, lo:lo + width])

    qa[...] = (proj(0, DA) * (HDA ** -0.5)).astype(BF16)
    for f32_ref, bf_ref, lo, transposed in ((ka, kab, DA, kv_transposed), (va, vab, 2 * DA, kv_transposed),
                                            (kb, kbb, 3 * DA + DB, False), (vb, vbb, 3 * DA + 2 * DB, False)):
        t = proj(lo, DA)
        f32_ref[...] = t.T if transposed else t
        bf_ref[...] = t.astype(BF16)
    qb[...] = (proj(3 * DA, DB) * (HDB ** -0.5)).astype(BF16)
    base = 3 * DA + 3 * DB
    for c in range(2):
        u[:, c * DC:(c + 1) * DC] = proj(base + c * DC, DC)
    vcb[...] = proj(base + 2 * DC, DC).astype(BF16)
    oc[...] = proj(base + 3 * DC, DC)
    g[...] = proj(Z_MAIN, LANES)


def _in_proj(x2d, w_bf16, tm, kv_transposed_batch=None):
    n = x2d.shape[0]
    widths_dtypes = [(DA, BF16), (DA, F32), (DA, F32), (DA, BF16), (DA, BF16),
                     (DB, BF16), (DB, F32), (DB, F32), (DB, BF16), (DB, BF16),
                     (2 * DC, F32), (DC, BF16), (DC, F32), (LANES, F32)]
    out_specs = [pl.BlockSpec((tm, w), lambda i: (i, 0)) for w, _ in widths_dtypes]
    out_shape = [jax.ShapeDtypeStruct((n, w), dt) for w, dt in widths_dtypes]
    if kv_transposed_batch is not None:
        tiles = n // kv_transposed_batch // tm
        for idx in (1, 2):
            out_specs[idx] = pl.BlockSpec((None, DA, tm), lambda i: (i // tiles, 0, i % tiles))
            out_shape[idx] = jax.ShapeDtypeStruct((kv_transposed_batch, DA, tiles * tm), F32)
    return pl.pallas_call(
        functools.partial(_in_proj_kernel, kv_transposed=kv_transposed_batch is not None),
        grid=(n // tm,),
        in_specs=[pl.BlockSpec((tm, D_MODEL), lambda i: (i, 0)),
                  pl.BlockSpec((D_MODEL, Z_COLS), lambda i: (0, 0))],
        out_specs=out_specs,
        out_shape=out_shape,
        compiler_params=_cparams("parallel"),
        name="in_proj",
    )(x2d, w_bf16)


def _sb_kernel(q_ref, k_ref, v_ref, o_ref, carry_ref, acc_ref, *, past, bq, bk, n_masked):
    i = pl.program_id(1)
    qs = _stack_heads(q_ref[...], HA, HDA)
    jj = lax.broadcasted_iota(jnp.int32, (bk, 2 * bk), 0)
    ss = lax.broadcasted_iota(jnp.int32, (bk, 2 * bk), 1)
    csum = jnp.where((ss >= bk) | (jj > ss), 1.0, 0.0).astype(BF16)
    csum2 = jnp.concatenate([csum, csum], axis=0)
    kb_last = (past + (i + 1) * bq - 2) // bk

    def block(kb, masked, first):
        start = pl.multiple_of(kb * bk, bk)
        s = _dot_nt(qs, k_ref[pl.ds(start, bk), :])
        cost = jnp.maximum(s, 0.0) + jnp.log(1.0 + jnp.exp(-jnp.abs(s)))
        if masked:
            row = lax.broadcasted_iota(jnp.int32, (HA * bq, bk), 0) & (bq - 1)
            mask = (start + lax.broadcasted_iota(jnp.int32, (HA * bq, bk), 1)) < past + i * bq + row
            stay = jnp.where(mask, cost, 0.0)
        else:
            stay = cost
        hi = stay.astype(BF16)
        lo = (stay - hi.astype(F32)).astype(BF16)
        cs = _dot(jnp.concatenate([hi, lo], axis=1), csum2)
        before = cs[:, :bk] if first else carry_ref[...] + cs[:, :bk]
        w = jnp.exp(s - cost - before)
        if masked:
            w = jnp.where(mask, w, 0.0)
        pv = _dot(w.astype(BF16), v_ref[pl.ds(start, bk), :])
        if first:
            carry = cs[:, bk:]
            acc_ref[...] = pv
        else:
            carry = carry_ref[...] + cs[:, bk:]
            acc_ref[...] += pv
        carry_ref[...] = carry
        return (jnp.min(carry) > SB_STAY_CUTOFF).astype(jnp.int32)

    done0 = block(kb_last, True, True)
    for d in range(1, n_masked):
        done0 = block(kb_last - d, True, False)

    def cond(state):
        kb, done = state
        return jnp.logical_and(kb >= 0, done == 0)

    def body(state):
        kb, _ = state
        return kb - 1, block(kb, False, False)

    lax.while_loop(cond, body, (kb_last - n_masked, done0))
    o_ref[...] = _unstack_heads(acc_ref[...], HA, HDA).astype(o_ref.dtype)


def _sb_attention(q_bf16, k_bf16, v_bf16, *, batch, tq, past, bq):
    bk = SB_BLOCK
    nq = tq // bq
    tk = k_bf16.shape[1]
    assert tk % bk == 0 and (past + tq - 2) // bk < tk // bk
    assert past % bk == 0 and (bq % bk == 0 or (nq == 1 and bq <= bk))
    n_masked = max(bq // bk, 1)
    return pl.pallas_call(
        functools.partial(_sb_kernel, past=past, bq=bq, bk=bk, n_masked=n_masked),
        grid=(batch, nq),
        in_specs=[pl.BlockSpec((bq, DA), lambda b, i: (b * nq + i, 0)),
                  pl.BlockSpec((None, tk, DA), lambda b, i: (b, 0, 0)),
                  pl.BlockSpec((None, tk, DA), lambda b, i: (b, 0, 0))],
        out_specs=pl.BlockSpec((bq, DA), lambda b, i: (b * nq + i, 0)),
        out_shape=jax.ShapeDtypeStruct((batch * tq, DA), BF16),
        scratch_shapes=[pltpu.VMEM((HA * bq, bk), F32), pltpu.VMEM((HA * bq, DA), F32)],
        compiler_params=_cparams("parallel", "arbitrary"),
        name="sb_attention",
    )(q_bf16, k_bf16, v_bf16)


def _band_kernel(q_ref, *refs, n_kb, rows_per_pass):
    k_refs, v_refs = refs[:n_kb], refs[n_kb:2 * n_kb]
    bias_ref, o_ref, s_ref, p_ref = refs[2 * n_kb:]
    j = pl.program_id(1)
    qs = _stack_heads(q_ref[...], HB, HDB)
    kbs = k_refs[0].shape[0]
    for i in range(n_kb):
        s = _dot_nt(qs, k_refs[i][...]) + bias_ref[:, i * kbs:(i + 1) * kbs]
        if i < n_kb - 1:
            s = s + jnp.where(j + i < n_kb - 1, -jnp.inf, 0.0)
        s_ref[:, i * kbs:(i + 1) * kbs] = s
    for r in range(0, s_ref.shape[0], rows_per_pass):
        s = s_ref[r:r + rows_per_pass, :]
        e = jnp.exp(s - jnp.max(s, axis=-1, keepdims=True))
        p_ref[r:r + rows_per_pass, :] = (e * (1.0 / jnp.sum(e, axis=-1, keepdims=True))).astype(BF16)
    acc = _dot(p_ref[:, 0:kbs], v_refs[0][...])
    for i in range(1, n_kb):
        acc = acc + _dot(p_ref[:, i * kbs:(i + 1) * kbs], v_refs[i][...])
    o_ref[...] = _unstack_heads(acc, HB, HDB).astype(o_ref.dtype)


def _band_attention(q_bf16, k_bf16, v_bf16, bias, *, batch, steps, gq, n_kb, kbs):
    blocks_per_batch = k_bf16.shape[0] // (batch * kbs)
    kv_specs = [pl.BlockSpec((kbs, DB), functools.partial(
        lambda b, j, i: (b * blocks_per_batch + jnp.maximum(j + i - (n_kb - 1), 0), 0), i=i)) for i in range(n_kb)]
    return pl.pallas_call(
        functools.partial(_band_kernel, n_kb=n_kb, rows_per_pass=min(128, HB * gq)),
        grid=(batch, steps),
        in_specs=[pl.BlockSpec((gq, DB), lambda b, j: (b * steps + j, 0))] + kv_specs + kv_specs
        + [pl.BlockSpec((HB * gq, n_kb * kbs), lambda b, j: (0, 0))],
        out_specs=pl.BlockSpec((gq, DB), lambda b, j: (b * steps + j, 0)),
        out_shape=jax.ShapeDtypeStruct((batch * steps * gq, DB), BF16),
        scratch_shapes=[pltpu.VMEM((HB * gq, n_kb * kbs), F32), pltpu.VMEM((HB * gq, n_kb * kbs), BF16)],
        compiler_params=_cparams("parallel", "arbitrary"),
        name="band_attention",
    )(q_bf16, *([k_bf16] * n_kb), *([v_bf16] * n_kb), bias)


def _mlstm_kernel(u_ref, vc_ref, oc_ref, g_ref, convw_ref, convb_ref, bif_ref, ng_ref,
                  cn0_ref, m0_ref, conv0_ref, o_ref, cn_ref, m_ref, ubuf, *, length):
    L = length
    c = pl.program_id(1)

    @pl.when(c == 0)
    def _():
        cn_ref[...] = cn0_ref[...]
        m_ref[...] = m0_ref[...]
        ubuf[0:8, :] = conv0_ref[...]

    ubuf[8:8 + L, :] = u_ref[...]
    y = convb_ref[...]
    for w in range(CONV_W):
        y = y + ubuf[8 - (CONV_W - 1) + w:8 - (CONV_W - 1) + w + L, :] * convw_ref[w:w + 1, :]
    tail = ubuf[L:L + 8, :]
    ubuf[0:8, :] = tail
    qk = y * jax.nn.sigmoid(y)

    g = g_ref[...] + bif_ref[...]
    lf_all, _ = _log_sigmoid_pair(g)
    r_i = lax.broadcasted_iota(jnp.int32, (L, L), 0)
    c_i = lax.broadcasted_iota(jnp.int32, (L, L), 1)
    causal = c_i <= r_i
    tril = jnp.where(causal, 1.0, 0.0).astype(BF16)
    hi, mid, lo = _split3(lf_all)
    b_all = _dot(tril, hi) + _dot(tril, mid) + _dot(tril, lo)
    lane = lax.broadcasted_iota(jnp.int32, (L, LANES), 1)
    mixed = jnp.where(lane < HC, g, b_all)
    eye = jnp.where(lax.broadcasted_iota(jnp.int32, (16, LANES), 0)
                    == lax.broadcasted_iota(jnp.int32, (16, LANES), 1), 1.0, 0.0).astype(BF16)
    hi, mid, lo = _split3(mixed)
    rows = _dot_nt(eye, hi) + _dot_nt(eye, mid) + _dot_nt(eye, lo)
    ones_col = jnp.where(lax.broadcasted_iota(jnp.int32, (L, HDC), 1) == 0, 1.0, 0.0).astype(BF16)

    for h in range(HC):
        li_row = rows[h:h + 1, :]
        b_row = rows[HC + h:HC + h + 1, :]
        li_col = g[:, h:h + 1]
        b_col = b_all[:, HC + h:HC + h + 1]
        dmat = jnp.where(causal, b_col - b_row + li_row, -jnp.inf)
        m_prev = m_ref[h][0:1, 0:1]
        inter = b_col + m_prev
        m_t = jnp.maximum(inter, jnp.max(dmat, axis=-1, keepdims=True))
        w_intra = jnp.exp(dmat - m_t)
        w_inter = jnp.exp(inter - m_t)
        qh = qk[:, h * HDC:(h + 1) * HDC].astype(BF16)
        kf = qk[:, DC + h * HDC:DC + (h + 1) * HDC] * (HDC ** -0.5)
        vaug = jnp.concatenate([vc_ref[:, h * HDC:(h + 1) * HDC], ones_col], axis=1)
        s = _dot_nt(qh, kf.astype(BF16)) * w_intra
        cn = cn_ref[h]
        qcn = _dot(qh, cn.astype(BF16))
        sv = _dot(s.astype(BF16), vaug[:, :HDC])
        num = w_inter * qcn[:, :HDC] + sv
        den = w_inter * qcn[:, HDC:HDC + 1] + jnp.sum(s, axis=-1, keepdims=True)
        hh = num / jnp.maximum(jnp.abs(den), jnp.exp(-m_t))
        m_new = m_t[L - 1:L, :]
        decay = jnp.exp(inter[L - 1:L, :] - m_new)
        w_k = jnp.exp(b_col[L - 1:L, :] - b_col + li_col - m_new)
        cn_ref[h] = decay * cn + _dot_tn((kf * w_k).astype(BF16), vaug)
        m_ref[h] = jnp.broadcast_to(m_new, (8, LANES))
        mu = jnp.mean(hh, axis=-1, keepdims=True)
        dev = hh - mu
        var = jnp.mean(dev * dev, axis=-1, keepdims=True)
        hn = dev * lax.rsqrt(var + LN_EPS) * ng_ref[:, h * HDC:(h + 1) * HDC]
        gate = jax.nn.sigmoid(oc_ref[:, h * HDC:(h + 1) * HDC])
        o_ref[:, h * HDC:(h + 1) * HDC] = (gate * hn).astype(o_ref.dtype)


def _mlstm(u, vc_bf16, oc, gates, conv_w, conv_b, bif_pad, norm_g, cn0, m0, conv0, *, batch, t, length):
    nc = t // length
    row = lambda w: pl.BlockSpec((length, w), lambda b, c: (b * nc + c, 0))
    const = lambda shape: pl.BlockSpec(shape, lambda b, c: (0,) * len(shape))
    per_batch = lambda shape: pl.BlockSpec((None,) + shape, lambda b, c: (b,) + (0,) * len(shape))
    return pl.pallas_call(
        functools.partial(_mlstm_kernel, length=length),
        grid=(batch, nc),
        in_specs=[row(2 * DC), row(DC), row(DC), row(LANES),
                  const((CONV_W, 2 * DC)), const((1, 2 * DC)), const((1, LANES)), const((1, DC)),
                  per_batch((HC, HDC, 2 * HDC)), per_batch((HC, 8, LANES)), per_batch((8, 2 * DC))],
        out_specs=[row(DC), per_batch((HC, HDC, 2 * HDC)), per_batch((HC, 8, LANES))],
        out_shape=[jax.ShapeDtypeStruct((batch * t, DC), BF16),
                   jax.ShapeDtypeStruct((batch, HC, HDC, 2 * HDC), F32),
                   jax.ShapeDtypeStruct((batch, HC, 8, LANES), F32)],
        scratch_shapes=[pltpu.VMEM((8 + length, 2 * DC), F32)],
        compiler_params=_cparams("parallel", "arbitrary"),
        name="mlstm",
    )(u, vc_bf16, oc, gates, conv_w, conv_b, bif_pad, norm_g, cn0, m0, conv0)


def _layer_norm(y, g, b):
    mu = jnp.mean(y, axis=-1, keepdims=True)
    dev = y - mu
    var = jnp.mean(dev * dev, axis=-1, keepdims=True)
    return dev * lax.rsqrt(var + LN_EPS) * g + b


def _out_proj_kernel(x_ref, a_ref, b_ref, c_ref, w_ref, g_ref, beta_ref, o_ref):
    mix = (_dot(a_ref[...], w_ref[0:DA, :]) + _dot(b_ref[...], w_ref[DA:DA + DB, :])
           + _dot(c_ref[...], w_ref[DA + DB:, :]))
    o_ref[...] = _layer_norm(ALPHA * x_ref[...] + mix, g_ref[...], beta_ref[...])


def _out_proj(x2d, a, b, c, w_bf16, g, beta, tm):
    n = x2d.shape[0]
    row = lambda w: pl.BlockSpec((tm, w), lambda i: (i, 0))
    const = lambda shape: pl.BlockSpec(shape, lambda i: (0, 0))
    return pl.pallas_call(
        _out_proj_kernel,
        grid=(n // tm,),
        in_specs=[row(D_MODEL), row(DA), row(DB), row(DC), const((D_MODEL, D_MODEL)),
                  const((1, D_MODEL)), const((1, D_MODEL))],
        out_specs=row(D_MODEL),
        out_shape=jax.ShapeDtypeStruct((n, D_MODEL), F32),
        compiler_params=_cparams("parallel"),
        name="out_proj_ln",
    )(x2d, a, b, c, w_bf16, g, beta)


def _route(logits):
    lane = lax.broadcasted_iota(jnp.int32, logits.shape, 1)
    big = jnp.int32(LANES)
    is_g = lane < N_GROUPS
    lg = jnp.where(is_g, logits, -jnp.inf)
    g_max = jnp.max(lg, axis=-1, keepdims=True)
    g_sel = jnp.min(jnp.where(lg == g_max, lane, big), axis=-1, keepdims=True)
    p_g = 1.0 / jnp.sum(jnp.where(is_g, jnp.exp(logits - g_max), 0.0), axis=-1, keepdims=True)
    e_lo = N_GROUPS + g_sel * E_PER_GROUP
    in_group = (lane >= e_lo) & (lane < e_lo + E_PER_GROUP)
    le = jnp.where(in_group, logits, -jnp.inf)
    v1 = jnp.max(le, axis=-1, keepdims=True)
    i1 = jnp.min(jnp.where(le == v1, lane, big), axis=-1, keepdims=True)
    le2 = jnp.where(lane == i1, -jnp.inf, le)
    v2 = jnp.max(le2, axis=-1, keepdims=True)
    i2 = jnp.min(jnp.where(le2 == v2, lane, big), axis=-1, keepdims=True)
    e2 = jnp.exp(v2 - v1)
    tot = 1.0 + e2
    comb = jnp.where(lane == i1, p_g * (1.0 / tot), 0.0) + jnp.where(lane == i2, p_g * (e2 / tot), 0.0)
    return comb, g_sel


def _moe_kernel(x_ref, rw_ref, rb_ref, wg_ref, wu_ref, wd_ref, g_ref, beta_ref, o_ref,
                xs_ref, combs_ref, acc_ref, pos_ref, tab_ref):
    step = pl.program_id(1)
    tm = x_ref.shape[0]
    n_sorted = xs_ref.shape[0]
    blk = MOE_BLOCK

    @pl.when(step == 0)
    def _():
        xb = x_ref[...].astype(BF16)
        comb, g_sel = _route(_dot(xb, rw_ref[...]) + rb_ref[...])
        lane = lax.broadcasted_iota(jnp.int32, (tm, LANES), 1)
        onehot = jnp.where(lane == g_sel, 1.0, 0.0)
        onehot_bf = onehot.astype(BF16)
        col = lax.broadcasted_iota(jnp.int32, (blk, tm), 1)
        row = lax.broadcasted_iota(jnp.int32, (blk, tm), 0)
        cum = jnp.concatenate(
            [_dot(jnp.where(col <= row + r, 1.0, 0.0).astype(BF16), onehot_bf) for r in range(0, tm, blk)], axis=0)
        count = cum[tm - 1:tm, :]
        padded = jnp.floor((count + (blk - 1)) * (1.0 / blk)) * blk
        lane1 = lax.broadcasted_iota(jnp.int32, (1, LANES), 1)
        offsets = jnp.zeros((1, LANES), F32)
        for grp in range(N_GROUPS):
            off = jnp.sum(jnp.where(lane1 < grp, padded, 0.0), axis=-1, keepdims=True)
            n_blocks = jnp.sum(jnp.where(lane1 == grp, padded, 0.0), axis=-1, keepdims=True) * (1.0 / blk)
            offsets = jnp.where(lane1 == grp, off, offsets)
            tab_ref[grp] = off.astype(jnp.int32)[0, 0]
            tab_ref[N_GROUPS + grp] = n_blocks.astype(jnp.int32)[0, 0]
        pos = jnp.sum(onehot * (offsets + cum), axis=-1, keepdims=True) - 1.0
        pos_ref[...] = pos.astype(jnp.int32)
        pos_hi = jnp.floor(pos * (1.0 / 256.0))
        digits = jnp.where(lane == 0, pos_hi, jnp.where(lane == 1, pos - 256.0 * pos_hi, 0.0)).astype(BF16)
        eye = jnp.where(lax.broadcasted_iota(jnp.int32, (16, LANES), 0)
                        == lax.broadcasted_iota(jnp.int32, (16, LANES), 1), 1.0, 0.0).astype(BF16)
        dig_rows = _dot_nt(eye, digits)
        pos_row = (dig_rows[0:1, :] * 256.0 + dig_rows[1:2, :]).astype(jnp.int32)
        c_hi = comb.astype(BF16)
        c_lo = (comb - c_hi.astype(F32)).astype(BF16)
        payload = jnp.concatenate([xb, c_hi, c_lo], axis=1)
        for r in range(0, n_sorted, blk):
            perm = jnp.where(row == pos_row - r, 1.0, 0.0).astype(BF16)
            moved = _dot(perm, payload)
            xs_ref[r:r + blk, :] = moved[:, :D_MODEL].astype(BF16)
            combs_ref[r:r + blk, :] = moved[:, D_MODEL:D_MODEL + LANES] + moved[:, D_MODEL + LANES:]
        acc_ref[...] = jnp.zeros_like(acc_ref)

    grp = (step * MOE_EXPERTS_PER_STEP) // E_PER_GROUP
    start = tab_ref[grp]
    n_blocks = tab_ref[N_GROUPS + grp]

    def expert_rows(r0, rows):
        xsb = xs_ref[pl.ds(r0, rows), :]
        combs = combs_ref[pl.ds(r0, rows), :]
        acc = acc_ref[pl.ds(r0, rows), :]
        lane_b = lax.broadcasted_iota(jnp.int32, (rows, LANES), 1)
        for j in range(MOE_EXPERTS_PER_STEP):
            gate = _dot(xsb, wg_ref[j])
            h = gate * jax.nn.sigmoid(gate) * _dot(xsb, wu_ref[j])
            y = _dot(h.astype(BF16), wd_ref[j])
            lane_e = N_GROUPS + step * MOE_EXPERTS_PER_STEP + j
            acc = acc + jnp.sum(jnp.where(lane_b == lane_e, combs, 0.0), axis=-1, keepdims=True) * y
        acc_ref[pl.ds(r0, rows), :] = acc

    def block_pair(b, carry):
        expert_rows(pl.multiple_of(start + b * (2 * blk), blk), 2 * blk)
        return carry

    lax.fori_loop(0, n_blocks // 2, block_pair, 0)

    @pl.when(n_blocks % 2 == 1)
    def _():
        expert_rows(pl.multiple_of(start + (n_blocks - 1) * blk, blk), blk)

    @pl.when(step == N_EXPERTS // MOE_EXPERTS_PER_STEP - 1)
    def _():
        for r in range(0, n_sorted, blk):
            xs_ref[r:r + blk, :] = acc_ref[r:r + blk, :].astype(BF16)
        lane_s = lax.broadcasted_iota(jnp.int32, (blk, n_sorted), 1)
        for r in range(0, tm, blk):
            back = jnp.where(lane_s == pos_ref[r:r + blk, :], 1.0, 0.0).astype(BF16)
            y = _dot(back, xs_ref[...])
            o_ref[r:r + blk, :] = _layer_norm(ALPHA * x_ref[r:r + blk, :] + y, g_ref[...], beta_ref[...])


def _moe(x2d, rw_bf16, rb, wg, wu, wd, g, beta, tm, layer):
    n = x2d.shape[0]
    n_sorted = tm + N_GROUPS * MOE_BLOCK
    per_step = MOE_EXPERTS_PER_STEP
    assert tm % MOE_BLOCK == 0 and n_sorted < 256 * 256 and E_PER_GROUP % per_step == 0
    const = lambda shape: pl.BlockSpec(shape, lambda i, e: (0, 0))
    return pl.pallas_call(
        _moe_kernel,
        grid=(n // tm, N_EXPERTS // per_step),
        in_specs=[pl.BlockSpec((tm, D_MODEL), lambda i, e: (i, 0)),
                  const((D_MODEL, LANES)), const((1, LANES)),
                  pl.BlockSpec((None, per_step, D_MODEL, D_EXPERT), lambda i, e: (layer, e, 0, 0)),
                  pl.BlockSpec((None, per_step, D_MODEL, D_EXPERT), lambda i, e: (layer, e, 0, 0)),
                  pl.BlockSpec((None, per_step, D_EXPERT, D_MODEL), lambda i, e: (layer, e, 0, 0)),
                  const((1, D_MODEL)), const((1, D_MODEL))],
        out_specs=pl.BlockSpec((tm, D_MODEL), lambda i, e: (i, 0)),
        out_shape=jax.ShapeDtypeStruct((n, D_MODEL), F32),
        scratch_shapes=[pltpu.VMEM((n_sorted, D_MODEL), BF16),
                        pltpu.VMEM((n_sorted, LANES), F32), pltpu.VMEM((n_sorted, D_MODEL), F32),
                        pltpu.VMEM((tm, 1), jnp.int32), pltpu.SMEM((2 * N_GROUPS,), jnp.int32)],
        compiler_params=_cparams("parallel", "arbitrary"),
        name="moe_ln",
    )(x2d, rw_bf16, rb, wg, wu, wd, g, beta)


def _band_bias(rel_table):
    n_clipped = BAND_W - 1 - REL_MAX
    ext = jnp.concatenate([rel_table[:, REL_MAX - (CHUNK - 1):],
                           jnp.broadcast_to(rel_table[:, -1:], (HB, n_clipped))], axis=1)
    rev = ext[:, ::-1]
    return jnp.stack([rev[:, CHUNK - 1 - i:CHUNK - 1 - i + BAND_W] for i in range(CHUNK)], axis=1).astype(F32)


def _band_bias_grouped(bias):
    g_n = BAND_GROUP
    tabs = [jnp.pad(bias, ((0, 0), (0, 0), (g * CHUNK, (g_n - 1 - g) * CHUNK)), constant_values=-jnp.inf)
            for g in range(g_n)]
    return jnp.stack(tabs, axis=1).reshape(HB * g_n * CHUNK, (g_n + BAND_CHUNKS) * CHUNK)


def _layer_weights(l, w_in, b_if, conv_w, conv_b, rel_table, c_norm_g, w_out, ln1_g, ln1_b,
                   router_g_w, router_g_b, router_e_w, router_e_b, exp_w_gate, exp_w_up, exp_w_down,
                   ln2_g, ln2_b):
    pad_cols = lambda a: jnp.pad(a, ((0, 0), (0, LANES - a.shape[1])))
    bias = _band_bias(rel_table[l])
    return dict(
        w_in=jnp.concatenate([w_in[l, :, :Z_MAIN], pad_cols(w_in[l, :, Z_MAIN:])], axis=1).astype(BF16),
        bif=pad_cols(b_if[l][None, :]),
        conv_w=conv_w[l], conv_b=conv_b[l][None, :],
        bias=bias.reshape(HB * CHUNK, BAND_W), bias_grouped=_band_bias_grouped(bias),
        norm_g=c_norm_g[l].reshape(1, DC),
        w_out=w_out[l].astype(BF16),
        ln1_g=ln1_g[l][None, :], ln1_b=ln1_b[l][None, :],
        rw=pad_cols(jnp.concatenate([router_g_w[l], router_e_w[l]], axis=1)).astype(BF16),
        rb=pad_cols(jnp.concatenate([router_g_b[l], router_e_b[l]])[None, :]),
        wg=exp_w_gate, wu=exp_w_up, wd=exp_w_down, layer=l,
        ln2_g=ln2_g[l][None, :], ln2_b=ln2_b[l][None, :],
    )


def _layer(x, w, cache, *, tm, tm_moe, sb_bq, length, band_keep):
    batch, t, _ = x.shape
    n = batch * t
    tm, tm_moe = min(tm, n), min(tm_moe, n)
    x2d = x.reshape(n, D_MODEL)
    qa, ka, va, ka_bf, va_bf, qb, kb, vb, kb_bf, vb_bf, u, vc_bf, oc, gates = _in_proj(
        x2d, w["w_in"], tm, kv_transposed_batch=batch if cache is None else None)
    per_batch = lambda a: a.reshape(batch, t, a.shape[-1])

    if cache is None:
        past = 0
        sb_k, sb_v = per_batch(ka_bf), per_batch(va_bf)
        gq = BAND_GROUP * CHUNK
        assert t % gq == 0 and BAND_CHUNKS * CHUNK % gq == 0
        band = dict(k=kb_bf, v=vb_bf, bias=w["bias_grouped"], steps=t // gq, gq=gq,
                    n_kb=BAND_CHUNKS * CHUNK // gq + 1, kbs=gq)
        cn0 = jnp.zeros((batch, HC, HDC, 2 * HDC), F32)
        m0 = jnp.zeros((batch, HC, 8, LANES), F32)
        conv0 = jnp.zeros((batch, 8, 2 * DC), F32)
    else:
        a_k, a_v, b_k, b_v, c0, n0, m_init, conv_init = cache
        past = a_k.shape[1]
        tk_pad = -(past + t) % SB_BLOCK
        sb_cat = lambda old, new: jnp.pad(
            jnp.concatenate([old.reshape(batch, past, DA).astype(BF16), per_batch(new)], axis=1),
            ((0, 0), (0, tk_pad), (0, 0)))
        sb_k, sb_v = sb_cat(a_k, ka_bf), sb_cat(a_v, va_bf)
        hist = b_k.shape[1]
        assert hist == BAND_CHUNKS * CHUNK and t == CHUNK and past % CHUNK == 0 and past >= hist
        band_cat = lambda old, new: jnp.concatenate(
            [old.reshape(batch, hist, DB).astype(BF16), per_batch(new)], axis=1).reshape(batch * BAND_W, DB)
        band = dict(k=band_cat(b_k, kb_bf), v=band_cat(b_v, vb_bf), bias=w["bias"], steps=1, gq=CHUNK,
                    n_kb=1, kbs=BAND_W)
        cn0 = jnp.concatenate([c0, n0[..., None], jnp.zeros((batch, HC, HDC, HDC - 1), F32)], axis=-1)
        m0 = jnp.broadcast_to(m_init[:, :, None, None], (batch, HC, 8, LANES))
        conv0 = jnp.pad(conv_init, ((0, 0), (8 - (CONV_W - 1), 0), (0, 0)))

    a_out = _sb_attention(qa, sb_k, sb_v, batch=batch, tq=t, past=past, bq=sb_bq)
    b_out = _band_attention(qb, band["k"], band["v"], band["bias"], batch=batch, steps=band["steps"],
                            gq=band["gq"], n_kb=band["n_kb"], kbs=band["kbs"])
    c_out, cn1, m1 = _mlstm(u, vc_bf, oc, gates, w["conv_w"], w["conv_b"], w["bif"], w["norm_g"],
                            cn0, m0, conv0, batch=batch, t=t, length=length)
    x1 = _out_proj(x2d, a_out, b_out, c_out, w["w_out"], w["ln1_g"], w["ln1_b"], tm)
    x2 = _moe(x1, w["rw"], w["rb"], w["wg"], w["wu"], w["wd"], w["ln2_g"], w["ln2_b"], tm_moe, w["layer"])
    if cache is None:
        heads = lambda a: a.reshape(batch, HA, HDA, t).transpose(0, 3, 1, 2)
    else:
        heads = lambda a: a.reshape(batch, t, HA, HDA)
    state = (heads(ka), heads(va),
             per_batch(kb)[:, t - band_keep:].reshape(batch, band_keep, HB, HDB),
             per_batch(vb)[:, t - band_keep:].reshape(batch, band_keep, HB, HDB),
             cn1[..., :HDC], cn1[..., HDC], m1[:, :, 0, 0], per_batch(u)[:, t - (CONV_W - 1):])
    return x2.reshape(batch, t, D_MODEL), state


def kernel(x_prompt, x_sample, cache_a_k, cache_a_v, cache_b_k, cache_b_v, state_c_C, state_c_n,
           state_c_m, state_c_conv, w_in, b_if, conv_w, conv_b, rel_table, c_norm_g, w_out, ln1_g, ln1_b,
           router_g_w, router_g_b, router_e_w, router_e_b, exp_w_gate, exp_w_up, exp_w_down, ln2_g, ln2_b):
    band_keep = cache_b_k.shape[2]
    xp, xs = x_prompt, x_sample
    prompt_states, sample_states = [], []
    experts_bf16 = tuple(a.astype(BF16) for a in (exp_w_gate, exp_w_up, exp_w_down))
    for l in range(DEPTH):
        w = _layer_weights(l, w_in, b_if, conv_w, conv_b, rel_table, c_norm_g, w_out, ln1_g, ln1_b,
                           router_g_w, router_g_b, router_e_w, router_e_b, *experts_bf16, ln2_g, ln2_b)
        xp, sp = _layer(xp, w, None, tm=512, tm_moe=1024, sb_bq=2 * SB_BLOCK, length=2 * CHUNK,
                        band_keep=min(band_keep, xp.shape[1]))
        prompt_states.append(sp)
        cache_l = (cache_a_k[l], cache_a_v[l], cache_b_k[l], cache_b_v[l],
                   state_c_C[l], state_c_n[l], state_c_m[l], state_c_conv[l])
        xs, ss = _layer(xs, w, cache_l, tm=512, tm_moe=512, sb_bq=CHUNK, length=CHUNK,
                        band_keep=xs.shape[1])
        sample_states.append(ss)
    p = [jnp.stack(s) for s in zip(*prompt_states)]
    s = [jnp.stack(s) for s in zip(*sample_states)]
    return (xp, xs, *p, *s)
```

```python
import functools

import jax
import jax.numpy as jnp
from jax import lax
from jax.experimental import pallas as pl
from jax.experimental.pallas import tpu as pltpu

F32 = jnp.float32
BF16 = jnp.bfloat16

D_MODEL = 1024
DEPTH = 2
CHUNK = 64
HA, HDA = 4, 64
HB, HDB = 4, 64
HC, HDC = 4, 128
DA, DB, DC = HA * HDA, HB * HDB, HC * HDC
BAND_CHUNKS = 8
BAND_W = (BAND_CHUNKS + 1) * CHUNK
REL_MAX = 128
CONV_W = 4
N_GROUPS = 4
E_PER_GROUP = 4
N_EXPERTS = N_GROUPS * E_PER_GROUP
D_EXPERT = 512
ALPHA = (2 * DEPTH) ** 0.25
LN_EPS = 1e-5

LANES = 128
Z_MAIN = 3 * DA + 3 * DB + 4 * DC
Z_COLS = Z_MAIN + LANES
SB_BLOCK = 128
SB_STAY_CUTOFF = 104.0
BAND_GROUP = 4
MOE_BLOCK = 128
MOE_EXPERTS_PER_STEP = 2
VMEM_LIMIT = 48 * 1024 * 1024


def _cparams(*sem):
    return pltpu.CompilerParams(dimension_semantics=sem, vmem_limit_bytes=VMEM_LIMIT)


def _split3(x):
    hi = x.astype(BF16)
    r1 = x - hi.astype(F32)
    mid = r1.astype(BF16)
    lo = (r1 - mid.astype(F32)).astype(BF16)
    return hi, mid, lo


def _dot(a, b):
    return jnp.dot(a, b, preferred_element_type=F32)


def _dot_nt(a, b):
    return lax.dot_general(a, b, (((1,), (1,)), ((), ())), preferred_element_type=F32)


def _dot_tn(a, b):
    return lax.dot_general(a, b, (((0,), (0,)), ((), ())), preferred_element_type=F32)


def _log_sigmoid_pair(z):
    t = jnp.log1p(jnp.exp(-jnp.abs(z)))
    return -(jnp.maximum(-z, 0.0) + t), -(jnp.maximum(z, 0.0) + t)


def _stack_heads(q, n_heads, width):
    lane = lax.broadcasted_iota(jnp.int32, q.shape, 1)
    zero = jnp.zeros_like(q)
    return jnp.concatenate(
        [jnp.where((lane >= h * width) & (lane < (h + 1) * width), q, zero) for h in range(n_heads)], axis=0)


def _unstack_heads(acc, n_heads, width):
    rows = acc.shape[0] // n_heads
    lane = lax.broadcasted_iota(jnp.int32, (rows, n_heads * width), 1)
    out = acc[0:rows]
    for h in range(1, n_heads):
        out = jnp.where(lane >= h * width, acc[h * rows:(h + 1) * rows], out)
    return out


def _in_proj_kernel(x_ref, w_ref, *refs, n_stacked):
    if n_stacked:
        prev_k, prev_v, *refs = refs
    qa, ka, va, kab, vab, qb, kb, vb, kbb, vbb, u, vcb, oc, g = refs
    xb = x_ref[...].astype(BF16)

    def proj(lo, width):
        return _dot(xb, w_ref[:, lo:lo + width])

    qa[...] = (proj(0, DA) * (HDA ** -0.5)).astype(BF16)
    for f32_ref, bf_ref, lo in ((ka, kab, DA), (va, vab, 2 * DA)):
        t = proj(lo, DA)
        bf_ref[...] = t.astype(BF16)
        if n_stacked is None:
            f32_ref[...] = t
        else:
            f32_ref[n_stacked] = t.T
    if n_stacked:
        ka[0:n_stacked] = prev_k[...]
        va[0:n_stacked] = prev_v[...]
    for f32_ref, bf_ref, lo in ((kb, kbb, 3 * DA + DB), (vb, vbb, 3 * DA + 2 * DB)):
        t = proj(lo, DA)
        f32_ref[...] = t
        bf_ref[...] = t.astype(BF16)
    qb[...] = (proj(3 * DA, DB) * (HDB ** -0.5)).astype(BF16)
    base = 3 * DA + 3 * DB
    for c in range(2):
        u[:, c * DC:(c + 1) * DC] = proj(base + c * DC, DC)
    vcb[...] = proj(base + 2 * DC, DC).astype(BF16)
    oc[...] = proj(base + 3 * DC, DC)
    g[...] = proj(Z_MAIN, LANES)


def _in_proj(x2d, w_bf16, tm, kv_stack=None):
    n = x2d.shape[0]
    widths_dtypes = [(DA, BF16), (DA, F32), (DA, F32), (DA, BF16), (DA, BF16),
                     (DB, BF16), (DB, F32), (DB, F32), (DB, BF16), (DB, BF16),
                     (2 * DC, F32), (DC, BF16), (DC, F32), (LANES, F32)]
    out_specs = [pl.BlockSpec((tm, w), lambda i: (i, 0)) for w, _ in widths_dtypes]
    out_shape = [jax.ShapeDtypeStruct((n, w), dt) for w, dt in widths_dtypes]
    in_specs = [pl.BlockSpec((tm, D_MODEL), lambda i: (i, 0)), pl.BlockSpec((D_MODEL, Z_COLS), lambda i: (0, 0))]
    operands = [x2d, w_bf16]
    n_stacked = None
    if kv_stack is not None:
        batch, prev_k, prev_v = kv_stack
        tiles = n // batch // tm
        n_stacked = 0 if prev_k is None else prev_k.shape[0]
        stack_spec = lambda layers: pl.BlockSpec((layers, None, DA, tm), lambda i: (0, i // tiles, 0, i % tiles))
        for idx in (1, 2):
            out_specs[idx] = stack_spec(n_stacked + 1)
            out_shape[idx] = jax.ShapeDtypeStruct((n_stacked + 1, batch, DA, tiles * tm), F32)
        if n_stacked:
            in_specs += [stack_spec(n_stacked)] * 2
            operands += [prev_k, prev_v]
    return pl.pallas_call(
        functools.partial(_in_proj_kernel, n_stacked=n_stacked),
        grid=(n // tm,),
        in_specs=in_specs,
        out_specs=out_specs,
        out_shape=out_shape,
        compiler_params=_cparams("parallel"),
        name="in_proj",
    )(*operands)


def _sb_kernel(q_ref, k_ref, v_ref, o_ref, carry_ref, acc_ref, *, past, bq, bk, n_masked):
    i = pl.program_id(1)
    qs = _stack_heads(q_ref[...], HA, HDA)
    jj = lax.broadcasted_iota(jnp.int32, (bk, 2 * bk), 0)
    ss = lax.broadcasted_iota(jnp.int32, (bk, 2 * bk), 1)
    csum = jnp.where((ss >= bk) | (jj > ss), 1.0, 0.0).astype(BF16)
    csum2 = jnp.concatenate([csum, csum], axis=0)
    kb_last = (past + (i + 1) * bq - 2) // bk

    def block(kb, masked, first):
        start = pl.multiple_of(kb * bk, bk)
        s = _dot_nt(qs, k_ref[pl.ds(start, bk), :])
        cost = jnp.maximum(s, 0.0) + jnp.log(1.0 + jnp.exp(-jnp.abs(s)))
        if masked:
            row = lax.broadcasted_iota(jnp.int32, (HA * bq, bk), 0) & (bq - 1)
            mask = (start + lax.broadcasted_iota(jnp.int32, (HA * bq, bk), 1)) < past + i * bq + row
            stay = jnp.where(mask, cost, 0.0)
        else:
            stay = cost
        hi = stay.astype(BF16)
        lo = (stay - hi.astype(F32)).astype(BF16)
        cs = _dot(jnp.concatenate([hi, lo], axis=1), csum2)
        before = cs[:, :bk] if first else carry_ref[...] + cs[:, :bk]
        w = jnp.exp(s - cost - before)
        if masked:
            w = jnp.where(mask, w, 0.0)
        pv = _dot(w.astype(BF16), v_ref[pl.ds(start, bk), :])
        if first:
            carry = cs[:, bk:]
            acc_ref[...] = pv
        else:
            carry = carry_ref[...] + cs[:, bk:]
            acc_ref[...] += pv
        carry_ref[...] = carry
        return (jnp.min(carry) > SB_STAY_CUTOFF).astype(jnp.int32)

    done0 = block(kb_last, True, True)
    for d in range(1, n_masked):
        done0 = block(kb_last - d, True, False)

    def cond(state):
        kb, done = state
        return jnp.logical_and(kb >= 0, done == 0)

    def body(state):
        kb, _ = state
        return kb - 1, block(kb, False, False)

    lax.while_loop(cond, body, (kb_last - n_masked, done0))
    o_ref[...] = _unstack_heads(acc_ref[...], HA, HDA).astype(o_ref.dtype)


def _sb_attention(q_bf16, k_bf16, v_bf16, *, batch, tq, past, bq):
    bk = SB_BLOCK
    nq = tq // bq
    tk = k_bf16.shape[1]
    assert tk % bk == 0 and (past + tq - 2) // bk < tk // bk
    assert past % bk == 0 and (bq % bk == 0 or (nq == 1 and bq <= bk))
    n_masked = max(bq // bk, 1)
    return pl.pallas_call(
        functools.partial(_sb_kernel, past=past, bq=bq, bk=bk, n_masked=n_masked),
        grid=(batch, nq),
        in_specs=[pl.BlockSpec((bq, DA), lambda b, i: (b * nq + i, 0)),
                  pl.BlockSpec((None, tk, DA), lambda b, i: (b, 0, 0)),
                  pl.BlockSpec((None, tk, DA), lambda b, i: (b, 0, 0))],
        out_specs=pl.BlockSpec((bq, DA), lambda b, i: (b * nq + i, 0)),
        out_shape=jax.ShapeDtypeStruct((batch * tq, DA), BF16),
        scratch_shapes=[pltpu.VMEM((HA * bq, bk), F32), pltpu.VMEM((HA * bq, DA), F32)],
        compiler_params=_cparams("parallel", "arbitrary"),
        name="sb_attention",
    )(q_bf16, k_bf16, v_bf16)


def _band_kernel(q_ref, *refs, n_kb, rows_per_pass):
    k_refs, v_refs = refs[:n_kb], refs[n_kb:2 * n_kb]
    bias_ref, o_ref, s_ref, p_ref = refs[2 * n_kb:]
    j = pl.program_id(1)
    qs = _stack_heads(q_ref[...], HB, HDB)
    kbs = k_refs[0].shape[0]
    for i in range(n_kb):
        s = _dot_nt(qs, k_refs[i][...]) + bias_ref[:, i * kbs:(i + 1) * kbs]
        if i < n_kb - 1:
            s = s + jnp.where(j + i < n_kb - 1, -jnp.inf, 0.0)
        s_ref[:, i * kbs:(i + 1) * kbs] = s
    half = s_ref.shape[0] // 2
    accs = []
    for h0 in (0, half):
        for r in range(h0, h0 + half, rows_per_pass):
            s = s_ref[r:r + rows_per_pass, :]
            e = jnp.exp(s - jnp.max(s, axis=-1, keepdims=True))
            p_ref[r:r + rows_per_pass, :] = (e * (1.0 / jnp.sum(e, axis=-1, keepdims=True))).astype(BF16)
        acc = _dot(p_ref[h0:h0 + half, 0:kbs], v_refs[0][...])
        for i in range(1, n_kb):
            acc = acc + _dot(p_ref[h0:h0 + half, i * kbs:(i + 1) * kbs], v_refs[i][...])
        accs.append(acc)
    o_ref[...] = _unstack_heads(jnp.concatenate(accs, axis=0), HB, HDB).astype(o_ref.dtype)


def _band_attention(q_bf16, k_bf16, v_bf16, bias, *, batch, steps, gq, n_kb, kbs):
    blocks_per_batch = k_bf16.shape[0] // (batch * kbs)
    kv_specs = [pl.BlockSpec((kbs, DB), functools.partial(
        lambda b, j, i: (b * blocks_per_batch + jnp.maximum(j + i - (n_kb - 1), 0), 0), i=i)) for i in range(n_kb)]
    return pl.pallas_call(
        functools.partial(_band_kernel, n_kb=n_kb, rows_per_pass=min(128, HB * gq)),
        grid=(batch, steps),
        in_specs=[pl.BlockSpec((gq, DB), lambda b, j: (b * steps + j, 0))] + kv_specs + kv_specs
        + [pl.BlockSpec((HB * gq, n_kb * kbs), lambda b, j: (0, 0))],
        out_specs=pl.BlockSpec((gq, DB), lambda b, j: (b * steps + j, 0)),
        out_shape=jax.ShapeDtypeStruct((batch * steps * gq, DB), BF16),
        scratch_shapes=[pltpu.VMEM((HB * gq, n_kb * kbs), F32), pltpu.VMEM((HB * gq, n_kb * kbs), BF16)],
        compiler_params=_cparams("parallel", "arbitrary"),
        name="band_attention",
    )(q_bf16, *([k_bf16] * n_kb), *([v_bf16] * n_kb), bias)


def _mlstm_kernel(u_ref, vc_ref, oc_ref, g_ref, convw_ref, convb_ref, bif_ref, ng_ref,
                  cn0_ref, m0_ref, conv0_ref, o_ref, cn_ref, m_ref, ubuf, *, length):
    L = length
    c = pl.program_id(1)

    @pl.when(c == 0)
    def _():
        cn_ref[...] = cn0_ref[...]
        m_ref[...] = m0_ref[...]
        ubuf[0:8, :] = conv0_ref[...]

    ubuf[8:8 + L, :] = u_ref[...]
    y = convb_ref[...]
    for w in range(CONV_W):
        y = y + ubuf[8 - (CONV_W - 1) + w:8 - (CONV_W - 1) + w + L, :] * convw_ref[w:w + 1, :]
    tail = ubuf[L:L + 8, :]
    ubuf[0:8, :] = tail
    qk = y * jax.nn.sigmoid(y)

    g = g_ref[...] + bif_ref[...]
    lf_all, _ = _log_sigmoid_pair(g)
    r_i = lax.broadcasted_iota(jnp.int32, (L, L), 0)
    c_i = lax.broadcasted_iota(jnp.int32, (L, L), 1)
    causal = c_i <= r_i
    tril = jnp.where(causal, 1.0, 0.0).astype(BF16)
    hi, mid, lo = _split3(lf_all)
    b_all = _dot(tril, hi) + _dot(tril, mid) + _dot(tril, lo)
    lane = lax.broadcasted_iota(jnp.int32, (L, LANES), 1)
    mixed = jnp.where(lane < HC, g, b_all)
    eye = jnp.where(lax.broadcasted_iota(jnp.int32, (16, LANES), 0)
                    == lax.broadcasted_iota(jnp.int32, (16, LANES), 1), 1.0, 0.0).astype(BF16)
    hi, mid, lo = _split3(mixed)
    rows = _dot_nt(eye, hi) + _dot_nt(eye, mid) + _dot_nt(eye, lo)
    ones_col = jnp.where(lax.broadcasted_iota(jnp.int32, (L, HDC), 1) == 0, 1.0, 0.0).astype(BF16)

    for h in range(HC):
        li_row = rows[h:h + 1, :]
        b_row = rows[HC + h:HC + h + 1, :]
        li_col = g[:, h:h + 1]
        b_col = b_all[:, HC + h:HC + h + 1]
        dmat = jnp.where(causal, b_col - b_row + li_row, -jnp.inf)
        m_prev = m_ref[h][0:1, 0:1]
        inter = b_col + m_prev
        m_t = jnp.maximum(inter, jnp.max(dmat, axis=-1, keepdims=True))
        w_intra = jnp.exp(dmat - m_t)
        w_inter = jnp.exp(inter - m_t)
        qh = qk[:, h * HDC:(h + 1) * HDC].astype(BF16)
        kf = qk[:, DC + h * HDC:DC + (h + 1) * HDC] * (HDC ** -0.5)
        vaug = jnp.concatenate([vc_ref[:, h * HDC:(h + 1) * HDC], ones_col], axis=1)
        s = _dot_nt(qh, kf.astype(BF16)) * w_intra
        cn = cn_ref[h]
        qcn = _dot(qh, cn.astype(BF16))
        sv = _dot(s.astype(BF16), vaug[:, :HDC])
        num = w_inter * qcn[:, :HDC] + sv
        den = w_inter * qcn[:, HDC:HDC + 1] + jnp.sum(s, axis=-1, keepdims=True)
        hh = num / jnp.maximum(jnp.abs(den), jnp.exp(-m_t))
        m_new = m_t[L - 1:L, :]
        decay = jnp.exp(inter[L - 1:L, :] - m_new)
        w_k = jnp.exp(b_col[L - 1:L, :] - b_col + li_col - m_new)
        cn_ref[h] = decay * cn + _dot_tn((kf * w_k).astype(BF16), vaug)
        m_ref[h] = jnp.broadcast_to(m_new, (8, LANES))
        mu = jnp.mean(hh, axis=-1, keepdims=True)
        dev = hh - mu
        var = jnp.mean(dev * dev, axis=-1, keepdims=True)
        hn = dev * lax.rsqrt(var + LN_EPS) * ng_ref[:, h * HDC:(h + 1) * HDC]
        gate = jax.nn.sigmoid(oc_ref[:, h * HDC:(h + 1) * HDC])
        o_ref[:, h * HDC:(h + 1) * HDC] = (gate * hn).astype(o_ref.dtype)


def _mlstm(u, vc_bf16, oc, gates, conv_w, conv_b, bif_pad, norm_g, cn0, m0, conv0, *, batch, t, length):
    nc = t // length
    row = lambda w: pl.BlockSpec((length, w), lambda b, c: (b * nc + c, 0))
    const = lambda shape: pl.BlockSpec(shape, lambda b, c: (0,) * len(shape))
    per_batch = lambda shape: pl.BlockSpec((None,) + shape, lambda b, c: (b,) + (0,) * len(shape))
    return pl.pallas_call(
        functools.partial(_mlstm_kernel, length=length),
        grid=(batch, nc),
        in_specs=[row(2 * DC), row(DC), row(DC), row(LANES),
                  const((CONV_W, 2 * DC)), const((1, 2 * DC)), const((1, LANES)), const((1, DC)),
                  per_batch((HC, HDC, 2 * HDC)), per_batch((HC, 8, LANES)), per_batch((8, 2 * DC))],
        out_specs=[row(DC), per_batch((HC, HDC, 2 * HDC)), per_batch((HC, 8, LANES))],
        out_shape=[jax.ShapeDtypeStruct((batch * t, DC), BF16),
                   jax.ShapeDtypeStruct((batch, HC, HDC, 2 * HDC), F32),
                   jax.ShapeDtypeStruct((batch, HC, 8, LANES), F32)],
        scratch_shapes=[pltpu.VMEM((8 + length, 2 * DC), F32)],
        compiler_params=_cparams("parallel", "arbitrary"),
        name="mlstm",
    )(u, vc_bf16, oc, gates, conv_w, conv_b, bif_pad, norm_g, cn0, m0, conv0)


def _layer_norm(y, g, b):
    mu = jnp.mean(y, axis=-1, keepdims=True)
    dev = y - mu
    var = jnp.mean(dev * dev, axis=-1, keepdims=True)
    return dev * lax.rsqrt(var + LN_EPS) * g + b


def _out_proj_kernel(x_ref, a_ref, b_ref, c_ref, w_ref, g_ref, beta_ref, o_ref):
    mix = (_dot(a_ref[...], w_ref[0:DA, :]) + _dot(b_ref[...], w_ref[DA:DA + DB, :])
           + _dot(c_ref[...], w_ref[DA + DB:, :]))
    o_ref[...] = _layer_norm(ALPHA * x_ref[...] + mix, g_ref[...], beta_ref[...])


def _out_proj(x2d, a, b, c, w_bf16, g, beta, tm):
    n = x2d.shape[0]
    row = lambda w: pl.BlockSpec((tm, w), lambda i: (i, 0))
    const = lambda shape: pl.BlockSpec(shape, lambda i: (0, 0))
    return pl.pallas_call(
        _out_proj_kernel,
        grid=(n // tm,),
        in_specs=[row(D_MODEL), row(DA), row(DB), row(DC), const((D_MODEL, D_MODEL)),
                  const((1, D_MODEL)), const((1, D_MODEL))],
        out_specs=row(D_MODEL),
        out_shape=jax.ShapeDtypeStruct((n, D_MODEL), F32),
        compiler_params=_cparams("parallel"),
        name="out_proj_ln",
    )(x2d, a, b, c, w_bf16, g, beta)


def _route(logits):
    lane = lax.broadcasted_iota(jnp.int32, logits.shape, 1)
    big = jnp.int32(LANES)
    is_g = lane < N_GROUPS
    lg = jnp.where(is_g, logits, -jnp.inf)
    g_max = jnp.max(lg, axis=-1, keepdims=True)
    g_sel = jnp.min(jnp.where(lg == g_max, lane, big), axis=-1, keepdims=True)
    p_g = 1.0 / jnp.sum(jnp.where(is_g, jnp.exp(logits - g_max), 0.0), axis=-1, keepdims=True)
    e_lo = N_GROUPS + g_sel * E_PER_GROUP
    in_group = (lane >= e_lo) & (lane < e_lo + E_PER_GROUP)
    le = jnp.where(in_group, logits, -jnp.inf)
    v1 = jnp.max(le, axis=-1, keepdims=True)
    i1 = jnp.min(jnp.where(le == v1, lane, big), axis=-1, keepdims=True)
    le2 = jnp.where(lane == i1, -jnp.inf, le)
    v2 = jnp.max(le2, axis=-1, keepdims=True)
    i2 = jnp.min(jnp.where(le2 == v2, lane, big), axis=-1, keepdims=True)
    e2 = jnp.exp(v2 - v1)
    tot = 1.0 + e2
    comb = jnp.where(lane == i1, p_g * (1.0 / tot), 0.0) + jnp.where(lane == i2, p_g * (e2 / tot), 0.0)
    return comb, g_sel


def _moe_kernel(x_ref, rw_ref, rb_ref, wg_ref, wu_ref, wd_ref, g_ref, beta_ref, o_ref,
                xs_ref, combs_ref, acc_ref, pos_ref, tab_ref):
    step = pl.program_id(1)
    tm = x_ref.shape[0]
    n_sorted = xs_ref.shape[0]
    blk = MOE_BLOCK

    @pl.when(step == 0)
    def _():
        xb = x_ref[...].astype(BF16)
        comb, g_sel = _route(_dot(xb, rw_ref[...]) + rb_ref[...])
        lane = lax.broadcasted_iota(jnp.int32, (tm, LANES), 1)
        onehot = jnp.where(lane == g_sel, 1.0, 0.0)
        onehot_bf = onehot.astype(BF16)
        col = lax.broadcasted_iota(jnp.int32, (blk, tm), 1)
        row = lax.broadcasted_iota(jnp.int32, (blk, tm), 0)
        cum = jnp.concatenate(
            [_dot(jnp.where(col <= row + r, 1.0, 0.0).astype(BF16), onehot_bf) for r in range(0, tm, blk)], axis=0)
        count = cum[tm - 1:tm, :]
        padded = jnp.floor((count + (blk - 1)) * (1.0 / blk)) * blk
        lane1 = lax.broadcasted_iota(jnp.int32, (1, LANES), 1)
        offsets = jnp.zeros((1, LANES), F32)
        for grp in range(N_GROUPS):
            off = jnp.sum(jnp.where(lane1 < grp, padded, 0.0), axis=-1, keepdims=True)
            n_blocks = jnp.sum(jnp.where(lane1 == grp, padded, 0.0), axis=-1, keepdims=True) * (1.0 / blk)
            offsets = jnp.where(lane1 == grp, off, offsets)
            tab_ref[grp] = off.astype(jnp.int32)[0, 0]
            tab_ref[N_GROUPS + grp] = n_blocks.astype(jnp.int32)[0, 0]
        pos = jnp.sum(onehot * (offsets + cum), axis=-1, keepdims=True) - 1.0
        pos_ref[...] = pos.astype(jnp.int32)
        pos_hi = jnp.floor(pos * (1.0 / 256.0))
        digits = jnp.where(lane == 0, pos_hi, jnp.where(lane == 1, pos - 256.0 * pos_hi, 0.0)).astype(BF16)
        eye = jnp.where(lax.broadcasted_iota(jnp.int32, (16, LANES), 0)
                        == lax.broadcasted_iota(jnp.int32, (16, LANES), 1), 1.0, 0.0).astype(BF16)
        dig_rows = _dot_nt(eye, digits)
        pos_row = (dig_rows[0:1, :] * 256.0 + dig_rows[1:2, :]).astype(jnp.int32)
        c_hi = comb.astype(BF16)
        c_lo = (comb - c_hi.astype(F32)).astype(BF16)
        payload = jnp.concatenate([xb, c_hi, c_lo], axis=1)
        for r in range(0, n_sorted, blk):
            perm = jnp.where(row == pos_row - r, 1.0, 0.0).astype(BF16)
            moved = _dot(perm, payload)
            xs_ref[r:r + blk, :] = moved[:, :D_MODEL].astype(BF16)
            combs_ref[r:r + blk, :] = moved[:, D_MODEL:D_MODEL + LANES] + moved[:, D_MODEL + LANES:]
        acc_ref[...] = jnp.zeros_like(acc_ref)

    grp = (step * MOE_EXPERTS_PER_STEP) // E_PER_GROUP
    start = tab_ref[grp]
    n_blocks = tab_ref[N_GROUPS + grp]

    def expert_rows(r0, rows):
        xsb = xs_ref[pl.ds(r0, rows), :]
        combs = combs_ref[pl.ds(r0, rows), :]
        acc = acc_ref[pl.ds(r0, rows), :]
        lane_b = lax.broadcasted_iota(jnp.int32, (rows, LANES), 1)
        for j in range(MOE_EXPERTS_PER_STEP):
            gate = _dot(xsb, wg_ref[j])
            h = gate * jax.nn.sigmoid(gate) * _dot(xsb, wu_ref[j])
            y = _dot(h.astype(BF16), wd_ref[j])
            lane_e = N_GROUPS + step * MOE_EXPERTS_PER_STEP + j
            acc = acc + jnp.sum(jnp.where(lane_b == lane_e, combs, 0.0), axis=-1, keepdims=True) * y
        acc_ref[pl.ds(r0, rows), :] = acc

    def block_pair(b, carry):
        expert_rows(pl.multiple_of(start + b * (2 * blk), blk), 2 * blk)
        return carry

    lax.fori_loop(0, n_blocks // 2, block_pair, 0)

    @pl.when(n_blocks % 2 == 1)
    def _():
        expert_rows(pl.multiple_of(start + (n_blocks - 1) * blk, blk), blk)

    @pl.when(step == N_EXPERTS // MOE_EXPERTS_PER_STEP - 1)
    def _():
        for r in range(0, n_sorted, blk):
            xs_ref[r:r + blk, :] = acc_ref[r:r + blk, :].astype(BF16)
        lane_s = lax.broadcasted_iota(jnp.int32, (blk, n_sorted), 1)
        for r in range(0, tm, blk):
            back = jnp.where(lane_s == pos_ref[r:r + blk, :], 1.0, 0.0).astype(BF16)
            y = _dot(back, xs_ref[...])
            o_ref[r:r + blk, :] = _layer_norm(ALPHA * x_ref[r:r + blk, :] + y, g_ref[...], beta_ref[...])


def _moe(x2d, rw_bf16, rb, wg, wu, wd, g, beta, tm, layer):
    n = x2d.shape[0]
    n_sorted = tm + N_GROUPS * MOE_BLOCK
    per_step = MOE_EXPERTS_PER_STEP
    assert tm % MOE_BLOCK == 0 and n_sorted < 256 * 256 and E_PER_GROUP % per_step == 0
    const = lambda shape: pl.BlockSpec(shape, lambda i, e: (0, 0))
    return pl.pallas_call(
        _moe_kernel,
        grid=(n // tm, N_EXPERTS // per_step),
        in_specs=[pl.BlockSpec((tm, D_MODEL), lambda i, e: (i, 0)),
                  const((D_MODEL, LANES)), const((1, LANES)),
                  pl.BlockSpec((None, per_step, D_MODEL, D_EXPERT), lambda i, e: (layer, e, 0, 0)),
                  pl.BlockSpec((None, per_step, D_MODEL, D_EXPERT), lambda i, e: (layer, e, 0, 0)),
                  pl.BlockSpec((None, per_step, D_EXPERT, D_MODEL), lambda i, e: (layer, e, 0, 0)),
                  const((1, D_MODEL)), const((1, D_MODEL))],
        out_specs=pl.BlockSpec((tm, D_MODEL), lambda i, e: (i, 0)),
        out_shape=jax.ShapeDtypeStruct((n, D_MODEL), F32),
        scratch_shapes=[pltpu.VMEM((n_sorted, D_MODEL), BF16),
                        pltpu.VMEM((n_sorted, LANES), F32), pltpu.VMEM((n_sorted, D_MODEL), F32),
                        pltpu.VMEM((tm, 1), jnp.int32), pltpu.SMEM((2 * N_GROUPS,), jnp.int32)],
        compiler_params=_cparams("parallel", "arbitrary"),
        name="moe_ln",
    )(x2d, rw_bf16, rb, wg, wu, wd, g, beta)


def _band_bias(rel_table):
    n_clipped = BAND_W - 1 - REL_MAX
    ext = jnp.concatenate([rel_table[:, REL_MAX - (CHUNK - 1):],
                           jnp.broadcast_to(rel_table[:, -1:], (HB, n_clipped))], axis=1)
    rev = ext[:, ::-1]
    return jnp.stack([rev[:, CHUNK - 1 - i:CHUNK - 1 - i + BAND_W] for i in range(CHUNK)], axis=1).astype(F32)


def _band_bias_grouped(bias):
    g_n = BAND_GROUP
    tabs = [jnp.pad(bias, ((0, 0), (0, 0), (g * CHUNK, (g_n - 1 - g) * CHUNK)), constant_values=-jnp.inf)
            for g in range(g_n)]
    return jnp.stack(tabs, axis=1).reshape(HB * g_n * CHUNK, (g_n + BAND_CHUNKS) * CHUNK)


def _layer_weights(l, w_in, b_if, conv_w, conv_b, rel_table, c_norm_g, w_out, ln1_g, ln1_b,
                   router_g_w, router_g_b, router_e_w, router_e_b, exp_w_gate, exp_w_up, exp_w_down,
                   ln2_g, ln2_b):
    pad_cols = lambda a: jnp.pad(a, ((0, 0), (0, LANES - a.shape[1])))
    bias = _band_bias(rel_table[l])
    return dict(
        w_in=jnp.concatenate([w_in[l, :, :Z_MAIN], pad_cols(w_in[l, :, Z_MAIN:])], axis=1).astype(BF16),
        bif=pad_cols(b_if[l][None, :]),
        conv_w=conv_w[l], conv_b=conv_b[l][None, :],
        bias=bias.reshape(HB * CHUNK, BAND_W), bias_grouped=_band_bias_grouped(bias),
        norm_g=c_norm_g[l].reshape(1, DC),
        w_out=w_out[l].astype(BF16),
        ln1_g=ln1_g[l][None, :], ln1_b=ln1_b[l][None, :],
        rw=pad_cols(jnp.concatenate([router_g_w[l], router_e_w[l]], axis=1)).astype(BF16),
        rb=pad_cols(jnp.concatenate([router_g_b[l], router_e_b[l]])[None, :]),
        wg=exp_w_gate, wu=exp_w_up, wd=exp_w_down, layer=l,
        ln2_g=ln2_g[l][None, :], ln2_b=ln2_b[l][None, :],
    )


def _layer(x, w, cache, *, tm, tm_moe, sb_bq, length, band_keep, kv_prev=(None, None)):
    batch, t, _ = x.shape
    n = batch * t
    tm, tm_moe = min(tm, n), min(tm_moe, n)
    x2d = x.reshape(n, D_MODEL)
    qa, ka, va, ka_bf, va_bf, qb, kb, vb, kb_bf, vb_bf, u, vc_bf, oc, gates = _in_proj(
        x2d, w["w_in"], tm, kv_stack=(batch, *kv_prev) if cache is None else None)
    per_batch = lambda a: a.reshape(batch, t, a.shape[-1])

    if cache is None:
        past = 0
        sb_k, sb_v = per_batch(ka_bf), per_batch(va_bf)
        gq = BAND_GROUP * CHUNK
        assert t % gq == 0 and BAND_CHUNKS * CHUNK % gq == 0
        band = dict(k=kb_bf, v=vb_bf, bias=w["bias_grouped"], steps=t // gq, gq=gq,
                    n_kb=BAND_CHUNKS * CHUNK // gq + 1, kbs=gq)
        cn0 = jnp.zeros((batch, HC, HDC, 2 * HDC), F32)
        m0 = jnp.zeros((batch, HC, 8, LANES), F32)
        conv0 = jnp.zeros((batch, 8, 2 * DC), F32)
    else:
        a_k, a_v, b_k, b_v, c0, n0, m_init, conv_init = cache
        past = a_k.shape[1]
        tk_pad = -(past + t) % SB_BLOCK
        sb_cat = lambda old, new: jnp.pad(
            jnp.concatenate([old.reshape(batch, past, DA).astype(BF16), per_batch(new)], axis=1),
            ((0, 0), (0, tk_pad), (0, 0)))
        sb_k, sb_v = sb_cat(a_k, ka_bf), sb_cat(a_v, va_bf)
        hist = b_k.shape[1]
        assert hist == BAND_CHUNKS * CHUNK and t == CHUNK and past % CHUNK == 0 and past >= hist
        band_cat = lambda old, new: jnp.concatenate(
            [old.reshape(batch, hist, DB).astype(BF16), per_batch(new)], axis=1).reshape(batch * BAND_W, DB)
        band = dict(k=band_cat(b_k, kb_bf), v=band_cat(b_v, vb_bf), bias=w["bias"], steps=1, gq=CHUNK,
                    n_kb=1, kbs=BAND_W)
        cn0 = jnp.concatenate([c0, n0[..., None], jnp.zeros((batch, HC, HDC, HDC - 1), F32)], axis=-1)
        m0 = jnp.broadcast_to(m_init[:, :, None, None], (batch, HC, 8, LANES))
        conv0 = jnp.pad(conv_init, ((0, 0), (8 - (CONV_W - 1), 0), (0, 0)))

    a_out = _sb_attention(qa, sb_k, sb_v, batch=batch, tq=t, past=past, bq=sb_bq)
    b_out = _band_attention(qb, band["k"], band["v"], band["bias"], batch=batch, steps=band["steps"],
                            gq=band["gq"], n_kb=band["n_kb"], kbs=band["kbs"])
    c_out, cn1, m1 = _mlstm(u, vc_bf, oc, gates, w["conv_w"], w["conv_b"], w["bif"], w["norm_g"],
                            cn0, m0, conv0, batch=batch, t=t, length=length)
    x1 = _out_proj(x2d, a_out, b_out, c_out, w["w_out"], w["ln1_g"], w["ln1_b"], tm)
    x2 = _moe(x1, w["rw"], w["rb"], w["wg"], w["wu"], w["wd"], w["ln2_g"], w["ln2_b"], tm_moe, w["layer"])
    heads = (lambda a: a) if cache is None else (lambda a: a.reshape(batch, t, HA, HDA))
    state = (heads(ka), heads(va),
             per_batch(kb)[:, t - band_keep:].reshape(batch, band_keep, HB, HDB),
             per_batch(vb)[:, t - band_keep:].reshape(batch, band_keep, HB, HDB),
             cn1[..., :HDC], cn1[..., HDC], m1[:, :, 0, 0], per_batch(u)[:, t - (CONV_W - 1):])
    return x2.reshape(batch, t, D_MODEL), state


def kernel(x_prompt, x_sample, cache_a_k, cache_a_v, cache_b_k, cache_b_v, state_c_C, state_c_n,
           state_c_m, state_c_conv, w_in, b_if, conv_w, conv_b, rel_table, c_norm_g, w_out, ln1_g, ln1_b,
           router_g_w, router_g_b, router_e_w, router_e_b, exp_w_gate, exp_w_up, exp_w_down, ln2_g, ln2_b):
    band_keep = cache_b_k.shape[2]
    xp, xs = x_prompt, x_sample
    prompt_states, sample_states = [], []
    kv_stack = (None, None)
    experts_bf16 = tuple(a.astype(BF16) for a in (exp_w_gate, exp_w_up, exp_w_down))
    for l in range(DEPTH):
        w = _layer_weights(l, w_in, b_if, conv_w, conv_b, rel_table, c_norm_g, w_out, ln1_g, ln1_b,
                           router_g_w, router_g_b, router_e_w, router_e_b, *experts_bf16, ln2_g, ln2_b)
        xp, sp = _layer(xp, w, None, tm=512, tm_moe=1024, sb_bq=2 * SB_BLOCK, length=2 * CHUNK,
                        band_keep=min(band_keep, xp.shape[1]), kv_prev=kv_stack)
        kv_stack = sp[:2]
        prompt_states.append(sp[2:])
        cache_l = (cache_a_k[l], cache_a_v[l], cache_b_k[l], cache_b_v[l],
                   state_c_C[l], state_c_n[l], state_c_m[l], state_c_conv[l])
        xs, ss = _layer(xs, w, cache_l, tm=512, tm_moe=512, sb_bq=CHUNK, length=CHUNK,
                        band_keep=xs.shape[1])
        sample_states.append(ss)
    p_kv = [a.reshape(DEPTH, a.shape[1], HA, HDA, a.shape[3]).transpose(0, 1, 4, 2, 3) for a in kv_stack]
    p = [jnp.stack(s) for s in zip(*prompt_states)]
    s = [jnp.stack(s) for s in zip(*sample_states)]
    return (xp, xs, *p_kv, *p, *s)
```

```python
import functools

import jax
import jax.numpy as jnp
from jax import lax
from jax.experimental import pallas as pl
from jax.experimental.pallas import tpu as pltpu

F32 = jnp.float32
BF16 = jnp.bfloat16

D_MODEL = 1024
DEPTH = 2
CHUNK = 64
HA, HDA = 4, 64
HB, HDB = 4, 64
HC, HDC = 4, 128
DA, DB, DC = HA * HDA, HB * HDB, HC * HDC
BAND_CHUNKS = 8
BAND_W = (BAND_CHUNKS + 1) * CHUNK
REL_MAX = 128
CONV_W = 4
N_GROUPS = 4
E_PER_GROUP = 4
N_EXPERTS = N_GROUPS * E_PER_GROUP
D_EXPERT = 512
ALPHA = (2 * DEPTH) ** 0.25
LN_EPS = 1e-5

LANES = 128
Z_MAIN = 3 * DA + 3 * DB + 4 * DC
Z_COLS = Z_MAIN + LANES
SB_BLOCK = 128
SB_STAY_CUTOFF = 104.0
BAND_GROUP = 4
MOE_BLOCK = 128
MOE_EXPERTS_PER_STEP = 2
VMEM_LIMIT = 48 * 1024 * 1024


def _cparams(*sem):
    return pltpu.CompilerParams(dimension_semantics=sem, vmem_limit_bytes=VMEM_LIMIT)


def _split3(x):
    hi = x.astype(BF16)
    r1 = x - hi.astype(F32)
    mid = r1.astype(BF16)
    lo = (r1 - mid.astype(F32)).astype(BF16)
    return hi, mid, lo


def _dot(a, b):
    return jnp.dot(a, b, preferred_element_type=F32)


def _dot_nt(a, b):
    return lax.dot_general(a, b, (((1,), (1,)), ((), ())), preferred_element_type=F32)


def _dot_tn(a, b):
    return lax.dot_general(a, b, (((0,), (0,)), ((), ())), preferred_element_type=F32)


def _mm(a, b, kind="nn"):
    f = {"nn": _dot, "nt": _dot_nt, "tn": _dot_tn}[kind]
    if a.dtype == BF16 and b.dtype == BF16:
        return f(a, b)
    a_hi = a.astype(BF16)
    a_lo = (a - a_hi.astype(F32)).astype(BF16)
    b_hi = b.astype(BF16)
    b_lo = (b - b_hi.astype(F32)).astype(BF16)
    return f(a_hi, b_hi) + f(a_hi, b_lo) + f(a_lo, b_hi)


def _log_sigmoid_pair(z):
    t = jnp.log1p(jnp.exp(-jnp.abs(z)))
    return -(jnp.maximum(-z, 0.0) + t), -(jnp.maximum(z, 0.0) + t)


def _stack_heads(q, n_heads, width):
    lane = lax.broadcasted_iota(jnp.int32, q.shape, 1)
    zero = jnp.zeros_like(q)
    return jnp.concatenate(
        [jnp.where((lane >= h * width) & (lane < (h + 1) * width), q, zero) for h in range(n_heads)], axis=0)


def _unstack_heads(acc, n_heads, width):
    rows = acc.shape[0] // n_heads
    lane = lax.broadcasted_iota(jnp.int32, (rows, n_heads * width), 1)
    out = acc[0:rows]
    for h in range(1, n_heads):
        out = jnp.where(lane >= h * width, acc[h * rows:(h + 1) * rows], out)
    return out


def _in_proj_kernel(x_ref, w_ref, *refs, n_stacked):
    if n_stacked:
        prev_k, prev_v, *refs = refs
    qa, ka, va, kab, vab, qb, kb, vb, kbb, vbb, u, vcb, oc, g = refs
    xb = x_ref[...].astype(w_ref.dtype)

    def proj(lo, width):
        return _mm(xb, w_ref[:, lo:lo + width])

    qa[...] = (proj(0, DA) * (HDA ** -0.5)).astype(qa.dtype)
    for f32_ref, bf_ref, lo in ((ka, kab, DA), (va, vab, 2 * DA)):
        t = proj(lo, DA)
        bf_ref[...] = t.astype(bf_ref.dtype)
        if n_stacked is None:
            f32_ref[...] = t
        else:
            f32_ref[n_stacked] = t.T
    if n_stacked:
        ka[0:n_stacked] = prev_k[...]
        va[0:n_stacked] = prev_v[...]
    for f32_ref, bf_ref, lo in ((kb, kbb, 3 * DA + DB), (vb, vbb, 3 * DA + 2 * DB)):
        t = proj(lo, DA)
        f32_ref[...] = t
        bf_ref[...] = t.astype(bf_ref.dtype)
    qb[...] = (proj(3 * DA, DB) * (HDB ** -0.5)).astype(qb.dtype)
    base = 3 * DA + 3 * DB
    for c in range(2):
        u[:, c * DC:(c + 1) * DC] = proj(base + c * DC, DC)
    vcb[...] = proj(base + 2 * DC, DC).astype(vcb.dtype)
    oc[...] = proj(base + 3 * DC, DC)
    g[...] = proj(Z_MAIN, LANES)


def _in_proj(x2d, w_bf16, tm, kv_stack=None):
    n = x2d.shape[0]
    narrow = w_bf16.dtype
    widths_dtypes = [(DA, narrow), (DA, F32), (DA, F32), (DA, narrow), (DA, narrow),
                     (DB, narrow), (DB, F32), (DB, F32), (DB, narrow), (DB, narrow),
                     (2 * DC, F32), (DC, narrow), (DC, F32), (LANES, F32)]
    out_specs = [pl.BlockSpec((tm, w), lambda i: (i, 0)) for w, _ in widths_dtypes]
    out_shape = [jax.ShapeDtypeStruct((n, w), dt) for w, dt in widths_dtypes]
    in_specs = [pl.BlockSpec((tm, D_MODEL), lambda i: (i, 0)), pl.BlockSpec((D_MODEL, Z_COLS), lambda i: (0, 0))]
    operands = [x2d, w_bf16]
    n_stacked = None
    if kv_stack is not None:
        batch, prev_k, prev_v = kv_stack
        tiles = n // batch // tm
        n_stacked = 0 if prev_k is None else prev_k.shape[0]
        stack_spec = lambda layers: pl.BlockSpec((layers, None, DA, tm), lambda i: (0, i // tiles, 0, i % tiles))
        for idx in (1, 2):
            out_specs[idx] = stack_spec(n_stacked + 1)
            out_shape[idx] = jax.ShapeDtypeStruct((n_stacked + 1, batch, DA, tiles * tm), F32)
        if n_stacked:
            in_specs += [stack_spec(n_stacked)] * 2
            operands += [prev_k, prev_v]
    return pl.pallas_call(
        functools.partial(_in_proj_kernel, n_stacked=n_stacked),
        grid=(n // tm,),
        in_specs=in_specs,
        out_specs=out_specs,
        out_shape=out_shape,
        compiler_params=_cparams("parallel"),
        name="in_proj",
    )(*operands)


def _sb_kernel(q_ref, k_ref, v_ref, o_ref, carry_ref, acc_ref, *, past, bq, bk, n_masked):
    i = pl.program_id(1)
    qs = _stack_heads(q_ref[...], HA, HDA)
    jj = lax.broadcasted_iota(jnp.int32, (bk, 2 * bk), 0)
    ss = lax.broadcasted_iota(jnp.int32, (bk, 2 * bk), 1)
    csum = jnp.where((ss >= bk) | (jj > ss), 1.0, 0.0).astype(BF16)
    csum2 = jnp.concatenate([csum, csum], axis=0)
    kb_last = (past + (i + 1) * bq - 2) // bk

    def block(kb, masked, first):
        start = pl.multiple_of(kb * bk, bk)
        s = _mm(qs, k_ref[pl.ds(start, bk), :], "nt")
        cost = jnp.maximum(s, 0.0) + jnp.log(1.0 + jnp.exp(-jnp.abs(s)))
        if masked:
            row = lax.broadcasted_iota(jnp.int32, (HA * bq, bk), 0) & (bq - 1)
            mask = (start + lax.broadcasted_iota(jnp.int32, (HA * bq, bk), 1)) < past + i * bq + row
            stay = jnp.where(mask, cost, 0.0)
        else:
            stay = cost
        hi = stay.astype(BF16)
        lo = (stay - hi.astype(F32)).astype(BF16)
        cs = _dot(jnp.concatenate([hi, lo], axis=1), csum2)
        before = cs[:, :bk] if first else carry_ref[...] + cs[:, :bk]
        w = jnp.exp(s - cost - before)
        if masked:
            w = jnp.where(mask, w, 0.0)
        pv = _mm(w.astype(v_ref.dtype), v_ref[pl.ds(start, bk), :])
        if first:
            carry = cs[:, bk:]
            acc_ref[...] = pv
        else:
            carry = carry_ref[...] + cs[:, bk:]
            acc_ref[...] += pv
        carry_ref[...] = carry
        return (jnp.min(carry) > SB_STAY_CUTOFF).astype(jnp.int32)

    done0 = block(kb_last, True, True)
    for d in range(1, n_masked):
        done0 = block(kb_last - d, True, False)

    def cond(state):
        kb, done = state
        return jnp.logical_and(kb >= 0, done == 0)

    def body(state):
        kb, _ = state
        return kb - 1, block(kb, False, False)

    lax.while_loop(cond, body, (kb_last - n_masked, done0))
    o_ref[...] = _unstack_heads(acc_ref[...], HA, HDA).astype(o_ref.dtype)


def _sb_attention(q_bf16, k_bf16, v_bf16, *, batch, tq, past, bq, out_dtype):
    bk = SB_BLOCK
    nq = tq // bq
    tk = k_bf16.shape[1]
    assert tk % bk == 0 and (past + tq - 2) // bk < tk // bk
    assert past % bk == 0 and (bq % bk == 0 or (nq == 1 and bq <= bk))
    n_masked = max(bq // bk, 1)
    return pl.pallas_call(
        functools.partial(_sb_kernel, past=past, bq=bq, bk=bk, n_masked=n_masked),
        grid=(batch, nq),
        in_specs=[pl.BlockSpec((bq, DA), lambda b, i: (b * nq + i, 0)),
                  pl.BlockSpec((None, tk, DA), lambda b, i: (b, 0, 0)),
                  pl.BlockSpec((None, tk, DA), lambda b, i: (b, 0, 0))],
        out_specs=pl.BlockSpec((bq, DA), lambda b, i: (b * nq + i, 0)),
        out_shape=jax.ShapeDtypeStruct((batch * tq, DA), out_dtype),
        scratch_shapes=[pltpu.VMEM((HA * bq, bk), F32), pltpu.VMEM((HA * bq, DA), F32)],
        compiler_params=_cparams("parallel", "arbitrary"),
        name="sb_attention",
    )(q_bf16, k_bf16, v_bf16)


def _band_kernel(q_ref, *refs, n_kb, rows_per_pass):
    k_refs, v_refs = refs[:n_kb], refs[n_kb:2 * n_kb]
    bias_ref, o_ref, s_ref, p_ref = refs[2 * n_kb:]
    j = pl.program_id(1)
    qs = _stack_heads(q_ref[...], HB, HDB)
    kbs = k_refs[0].shape[0]
    for i in range(n_kb):
        s = _mm(qs, k_refs[i][...], "nt") + bias_ref[:, i * kbs:(i + 1) * kbs]
        if i < n_kb - 1:
            s = s + jnp.where(j + i < n_kb - 1, -jnp.inf, 0.0)
        s_ref[:, i * kbs:(i + 1) * kbs] = s
    half = s_ref.shape[0] // 2
    accs = []
    for h0 in (0, half):
        for r in range(h0, h0 + half, rows_per_pass):
            s = s_ref[r:r + rows_per_pass, :]
            e = jnp.exp(s - jnp.max(s, axis=-1, keepdims=True))
            p_ref[r:r + rows_per_pass, :] = (e * (1.0 / jnp.sum(e, axis=-1, keepdims=True))).astype(p_ref.dtype)
        acc = _mm(p_ref[h0:h0 + half, 0:kbs], v_refs[0][...])
        for i in range(1, n_kb):
            acc = acc + _mm(p_ref[h0:h0 + half, i * kbs:(i + 1) * kbs], v_refs[i][...])
        accs.append(acc)
    o_ref[...] = _unstack_heads(jnp.concatenate(accs, axis=0), HB, HDB).astype(o_ref.dtype)


def _band_attention(q_bf16, k_bf16, v_bf16, bias, *, batch, steps, gq, n_kb, kbs, out_dtype):
    blocks_per_batch = k_bf16.shape[0] // (batch * kbs)
    kv_specs = [pl.BlockSpec((kbs, DB), functools.partial(
        lambda b, j, i: (b * blocks_per_batch + jnp.maximum(j + i - (n_kb - 1), 0), 0), i=i)) for i in range(n_kb)]
    return pl.pallas_call(
        functools.partial(_band_kernel, n_kb=n_kb, rows_per_pass=min(128, HB * gq)),
        grid=(batch, steps),
        in_specs=[pl.BlockSpec((gq, DB), lambda b, j: (b * steps + j, 0))] + kv_specs + kv_specs
        + [pl.BlockSpec((HB * gq, n_kb * kbs), lambda b, j: (0, 0))],
        out_specs=pl.BlockSpec((gq, DB), lambda b, j: (b * steps + j, 0)),
        out_shape=jax.ShapeDtypeStruct((batch * steps * gq, DB), out_dtype),
        scratch_shapes=[pltpu.VMEM((HB * gq, n_kb * kbs), F32), pltpu.VMEM((HB * gq, n_kb * kbs), v_bf16.dtype)],
        compiler_params=_cparams("parallel", "arbitrary"),
        name="band_attention",
    )(q_bf16, *([k_bf16] * n_kb), *([v_bf16] * n_kb), bias)


def _mlstm_kernel(u_ref, vc_ref, oc_ref, g_ref, convw_ref, convb_ref, bif_ref, ng_ref,
                  cn0_ref, m0_ref, conv0_ref, o_ref, cn_ref, m_ref, ubuf, *, length):
    L = length
    c = pl.program_id(1)

    @pl.when(c == 0)
    def _():
        cn_ref[...] = cn0_ref[...]
        m_ref[...] = m0_ref[...]
        ubuf[0:8, :] = conv0_ref[...]

    ubuf[8:8 + L, :] = u_ref[...]
    y = convb_ref[...]
    for w in range(CONV_W):
        y = y + ubuf[8 - (CONV_W - 1) + w:8 - (CONV_W - 1) + w + L, :] * convw_ref[w:w + 1, :]
    tail = ubuf[L:L + 8, :]
    ubuf[0:8, :] = tail
    qk = y * jax.nn.sigmoid(y)

    g = g_ref[...] + bif_ref[...]
    lf_all, _ = _log_sigmoid_pair(g)
    r_i = lax.broadcasted_iota(jnp.int32, (L, L), 0)
    c_i = lax.broadcasted_iota(jnp.int32, (L, L), 1)
    causal = c_i <= r_i
    tril = jnp.where(causal, 1.0, 0.0).astype(BF16)
    hi, mid, lo = _split3(lf_all)
    b_all = _dot(tril, hi) + _dot(tril, mid) + _dot(tril, lo)
    lane = lax.broadcasted_iota(jnp.int32, (L, LANES), 1)
    mixed = jnp.where(lane < HC, g, b_all)
    eye = jnp.where(lax.broadcasted_iota(jnp.int32, (16, LANES), 0)
                    == lax.broadcasted_iota(jnp.int32, (16, LANES), 1), 1.0, 0.0).astype(BF16)
    hi, mid, lo = _split3(mixed)
    rows = _dot_nt(eye, hi) + _dot_nt(eye, mid) + _dot_nt(eye, lo)
    narrow = vc_ref.dtype
    ones_col = jnp.where(lax.broadcasted_iota(jnp.int32, (L, HDC), 1) == 0, 1.0, 0.0).astype(narrow)

    for h in range(HC):
        li_row = rows[h:h + 1, :]
        b_row = rows[HC + h:HC + h + 1, :]
        li_col = g[:, h:h + 1]
        b_col = b_all[:, HC + h:HC + h + 1]
        dmat = jnp.where(causal, b_col - b_row + li_row, -jnp.inf)
        m_prev = m_ref[h][0:1, 0:1]
        inter = b_col + m_prev
        m_t = jnp.maximum(inter, jnp.max(dmat, axis=-1, keepdims=True))
        w_intra = jnp.exp(dmat - m_t)
        w_inter = jnp.exp(inter - m_t)
        qh = qk[:, h * HDC:(h + 1) * HDC].astype(narrow)
        kf = qk[:, DC + h * HDC:DC + (h + 1) * HDC] * (HDC ** -0.5)
        vaug = jnp.concatenate([vc_ref[:, h * HDC:(h + 1) * HDC], ones_col], axis=1)
        s = _mm(qh, kf.astype(narrow), "nt") * w_intra
        cn = cn_ref[h]
        qcn = _mm(qh, cn.astype(narrow))
        sv = _mm(s.astype(narrow), vaug[:, :HDC])
        num = w_inter * qcn[:, :HDC] + sv
        den = w_inter * qcn[:, HDC:HDC + 1] + jnp.sum(s, axis=-1, keepdims=True)
        hh = num / jnp.maximum(jnp.abs(den), jnp.exp(-m_t))
        m_new = m_t[L - 1:L, :]
        decay = jnp.exp(inter[L - 1:L, :] - m_new)
        w_k = jnp.exp(b_col[L - 1:L, :] - b_col + li_col - m_new)
        cn_ref[h] = decay * cn + _mm((kf * w_k).astype(narrow), vaug, "tn")
        m_ref[h] = jnp.broadcast_to(m_new, (8, LANES))
        mu = jnp.mean(hh, axis=-1, keepdims=True)
        dev = hh - mu
        var = jnp.mean(dev * dev, axis=-1, keepdims=True)
        hn = dev * lax.rsqrt(var + LN_EPS) * ng_ref[:, h * HDC:(h + 1) * HDC]
        gate = jax.nn.sigmoid(oc_ref[:, h * HDC:(h + 1) * HDC])
        o_ref[:, h * HDC:(h + 1) * HDC] = (gate * hn).astype(o_ref.dtype)


def _mlstm(u, vc_bf16, oc, gates, conv_w, conv_b, bif_pad, norm_g, cn0, m0, conv0, *, batch, t, length,
           out_dtype):
    nc = t // length
    row = lambda w: pl.BlockSpec((length, w), lambda b, c: (b * nc + c, 0))
    const = lambda shape: pl.BlockSpec(shape, lambda b, c: (0,) * len(shape))
    per_batch = lambda shape: pl.BlockSpec((None,) + shape, lambda b, c: (b,) + (0,) * len(shape))
    return pl.pallas_call(
        functools.partial(_mlstm_kernel, length=length),
        grid=(batch, nc),
        in_specs=[row(2 * DC), row(DC), row(DC), row(LANES),
                  const((CONV_W, 2 * DC)), const((1, 2 * DC)), const((1, LANES)), const((1, DC)),
                  per_batch((HC, HDC, 2 * HDC)), per_batch((HC, 8, LANES)), per_batch((8, 2 * DC))],
        out_specs=[row(DC), per_batch((HC, HDC, 2 * HDC)), per_batch((HC, 8, LANES))],
        out_shape=[jax.ShapeDtypeStruct((batch * t, DC), out_dtype),
                   jax.ShapeDtypeStruct((batch, HC, HDC, 2 * HDC), F32),
                   jax.ShapeDtypeStruct((batch, HC, 8, LANES), F32)],
        scratch_shapes=[pltpu.VMEM((8 + length, 2 * DC), F32)],
        compiler_params=_cparams("parallel", "arbitrary"),
        name="mlstm",
    )(u, vc_bf16, oc, gates, conv_w, conv_b, bif_pad, norm_g, cn0, m0, conv0)


def _layer_norm(y, g, b):
    mu = jnp.mean(y, axis=-1, keepdims=True)
    dev = y - mu
    var = jnp.mean(dev * dev, axis=-1, keepdims=True)
    return dev * lax.rsqrt(var + LN_EPS) * g + b


def _out_proj_kernel(x_ref, a_ref, b_ref, c_ref, w_ref, g_ref, beta_ref, o_ref):
    mix = (_mm(a_ref[...], w_ref[0:DA, :]) + _mm(b_ref[...], w_ref[DA:DA + DB, :])
           + _mm(c_ref[...], w_ref[DA + DB:, :]))
    o_ref[...] = _layer_norm(ALPHA * x_ref[...] + mix, g_ref[...], beta_ref[...])


def _out_proj(x2d, a, b, c, w, g, beta, tm):
    assert a.dtype == b.dtype == c.dtype == w.dtype
    n = x2d.shape[0]
    row = lambda w: pl.BlockSpec((tm, w), lambda i: (i, 0))
    const = lambda shape: pl.BlockSpec(shape, lambda i: (0, 0))
    return pl.pallas_call(
        _out_proj_kernel,
        grid=(n // tm,),
        in_specs=[row(D_MODEL), row(DA), row(DB), row(DC), const((D_MODEL, D_MODEL)),
                  const((1, D_MODEL)), const((1, D_MODEL))],
        out_specs=row(D_MODEL),
        out_shape=jax.ShapeDtypeStruct((n, D_MODEL), F32),
        compiler_params=_cparams("parallel"),
        name="out_proj_ln",
    )(x2d, a, b, c, w, g, beta)


def _route(logits):
    lane = lax.broadcasted_iota(jnp.int32, logits.shape, 1)
    big = jnp.int32(LANES)
    is_g = lane < N_GROUPS
    lg = jnp.where(is_g, logits, -jnp.inf)
    g_max = jnp.max(lg, axis=-1, keepdims=True)
    g_sel = jnp.min(jnp.where(lg == g_max, lane, big), axis=-1, keepdims=True)
    p_g = 1.0 / jnp.sum(jnp.where(is_g, jnp.exp(logits - g_max), 0.0), axis=-1, keepdims=True)
    e_lo = N_GROUPS + g_sel * E_PER_GROUP
    in_group = (lane >= e_lo) & (lane < e_lo + E_PER_GROUP)
    le = jnp.where(in_group, logits, -jnp.inf)
    v1 = jnp.max(le, axis=-1, keepdims=True)
    i1 = jnp.min(jnp.where(le == v1, lane, big), axis=-1, keepdims=True)
    le2 = jnp.where(lane == i1, -jnp.inf, le)
    v2 = jnp.max(le2, axis=-1, keepdims=True)
    i2 = jnp.min(jnp.where(le2 == v2, lane, big), axis=-1, keepdims=True)
    e2 = jnp.exp(v2 - v1)
    tot = 1.0 + e2
    comb = jnp.where(lane == i1, p_g * (1.0 / tot), 0.0) + jnp.where(lane == i2, p_g * (e2 / tot), 0.0)
    return comb, g_sel


def _moe_kernel(x_ref, rw_ref, rb_ref, wg_ref, wu_ref, wd_ref, g_ref, beta_ref, o_ref,
                xs_ref, combs_ref, acc_ref, pos_ref, tab_ref):
    step = pl.program_id(1)
    tm = x_ref.shape[0]
    n_sorted = xs_ref.shape[0]
    blk = MOE_BLOCK

    @pl.when(step == 0)
    def _():
        xb = x_ref[...].astype(BF16)
        logits = _mm(x_ref[...].astype(rw_ref.dtype), rw_ref[...])
        comb, g_sel = _route(logits + rb_ref[...])
        lane = lax.broadcasted_iota(jnp.int32, (tm, LANES), 1)
        onehot = jnp.where(lane == g_sel, 1.0, 0.0)
        onehot_bf = onehot.astype(BF16)
        col = lax.broadcasted_iota(jnp.int32, (blk, tm), 1)
        row = lax.broadcasted_iota(jnp.int32, (blk, tm), 0)
        cum = jnp.concatenate(
            [_dot(jnp.where(col <= row + r, 1.0, 0.0).astype(BF16), onehot_bf) for r in range(0, tm, blk)], axis=0)
        count = cum[tm - 1:tm, :]
        padded = jnp.floor((count + (blk - 1)) * (1.0 / blk)) * blk
        lane1 = lax.broadcasted_iota(jnp.int32, (1, LANES), 1)
        offsets = jnp.zeros((1, LANES), F32)
        for grp in range(N_GROUPS):
            off = jnp.sum(jnp.where(lane1 < grp, padded, 0.0), axis=-1, keepdims=True)
            n_blocks = jnp.sum(jnp.where(lane1 == grp, padded, 0.0), axis=-1, keepdims=True) * (1.0 / blk)
            offsets = jnp.where(lane1 == grp, off, offsets)
            tab_ref[grp] = off.astype(jnp.int32)[0, 0]
            tab_ref[N_GROUPS + grp] = n_blocks.astype(jnp.int32)[0, 0]
        pos = jnp.sum(onehot * (offsets + cum), axis=-1, keepdims=True) - 1.0
        pos_ref[...] = pos.astype(jnp.int32)
        pos_hi = jnp.floor(pos * (1.0 / 256.0))
        digits = jnp.where(lane == 0, pos_hi, jnp.where(lane == 1, pos - 256.0 * pos_hi, 0.0)).astype(BF16)
        eye = jnp.where(lax.broadcasted_iota(jnp.int32, (16, LANES), 0)
                        == lax.broadcasted_iota(jnp.int32, (16, LANES), 1), 1.0, 0.0).astype(BF16)
        dig_rows = _dot_nt(eye, digits)
        pos_row = (dig_rows[0:1, :] * 256.0 + dig_rows[1:2, :]).astype(jnp.int32)
        c_hi = comb.astype(BF16)
        c_lo = (comb - c_hi.astype(F32)).astype(BF16)
        payload = jnp.concatenate([xb, c_hi, c_lo], axis=1)
        for r in range(0, n_sorted, blk):
            perm = jnp.where(row == pos_row - r, 1.0, 0.0).astype(BF16)
            moved = _dot(perm, payload)
            xs_ref[r:r + blk, :] = moved[:, :D_MODEL].astype(BF16)
            combs_ref[r:r + blk, :] = moved[:, D_MODEL:D_MODEL + LANES] + moved[:, D_MODEL + LANES:]
        acc_ref[...] = jnp.zeros_like(acc_ref)

    grp = (step * MOE_EXPERTS_PER_STEP) // E_PER_GROUP
    start = tab_ref[grp]
    n_blocks = tab_ref[N_GROUPS + grp]

    def expert_rows(r0, rows):
        xsb = xs_ref[pl.ds(r0, rows), :]
        combs = combs_ref[pl.ds(r0, rows), :]
        acc = acc_ref[pl.ds(r0, rows), :]
        lane_b = lax.broadcasted_iota(jnp.int32, (rows, LANES), 1)
        for j in range(MOE_EXPERTS_PER_STEP):
            gate = _dot(xsb, wg_ref[j])
            h = gate * jax.nn.sigmoid(gate) * _dot(xsb, wu_ref[j])
            y = _dot(h.astype(BF16), wd_ref[j])
            lane_e = N_GROUPS + step * MOE_EXPERTS_PER_STEP + j
            acc = acc + jnp.sum(jnp.where(lane_b == lane_e, combs, 0.0), axis=-1, keepdims=True) * y
        acc_ref[pl.ds(r0, rows), :] = acc

    def block_pair(b, carry):
        expert_rows(pl.multiple_of(start + b * (2 * blk), blk), 2 * blk)
        return carry

    lax.fori_loop(0, n_blocks // 2, block_pair, 0)

    @pl.when(n_blocks % 2 == 1)
    def _():
        expert_rows(pl.multiple_of(start + (n_blocks - 1) * blk, blk), blk)

    @pl.when(step == N_EXPERTS // MOE_EXPERTS_PER_STEP - 1)
    def _():
        for r in range(0, n_sorted, blk):
            xs_ref[r:r + blk, :] = acc_ref[r:r + blk, :].astype(BF16)
        lane_s = lax.broadcasted_iota(jnp.int32, (blk, n_sorted), 1)
        for r in range(0, tm, blk):
            back = jnp.where(lane_s == pos_ref[r:r + blk, :], 1.0, 0.0).astype(BF16)
            y = _dot(back, xs_ref[...])
            o_ref[r:r + blk, :] = _layer_norm(ALPHA * x_ref[r:r + blk, :] + y, g_ref[...], beta_ref[...])


def _moe(x2d, rw, rb, wg, wu, wd, g, beta, tm, layer):
    n = x2d.shape[0]
    n_sorted = tm + N_GROUPS * MOE_BLOCK
    per_step = MOE_EXPERTS_PER_STEP
    assert tm % MOE_BLOCK == 0 and n_sorted < 256 * 256 and E_PER_GROUP % per_step == 0
    const = lambda shape: pl.BlockSpec(shape, lambda i, e: (0, 0))
    return pl.pallas_call(
        _moe_kernel,
        grid=(n // tm, N_EXPERTS // per_step),
        in_specs=[pl.BlockSpec((tm, D_MODEL), lambda i, e: (i, 0)),
                  const((D_MODEL, LANES)), const((1, LANES)),
                  pl.BlockSpec((None, per_step, D_MODEL, D_EXPERT), lambda i, e: (layer, e, 0, 0)),
                  pl.BlockSpec((None, per_step, D_MODEL, D_EXPERT), lambda i, e: (layer, e, 0, 0)),
                  pl.BlockSpec((None, per_step, D_EXPERT, D_MODEL), lambda i, e: (layer, e, 0, 0)),
                  const((1, D_MODEL)), const((1, D_MODEL))],
        out_specs=pl.BlockSpec((tm, D_MODEL), lambda i, e: (i, 0)),
        out_shape=jax.ShapeDtypeStruct((n, D_MODEL), F32),
        scratch_shapes=[pltpu.VMEM((n_sorted, D_MODEL), BF16),
                        pltpu.VMEM((n_sorted, LANES), F32), pltpu.VMEM((n_sorted, D_MODEL), F32),
                        pltpu.VMEM((tm, 1), jnp.int32), pltpu.SMEM((2 * N_GROUPS,), jnp.int32)],
        compiler_params=_cparams("parallel", "arbitrary"),
        name="moe_ln",
    )(x2d, rw, rb, wg, wu, wd, g, beta)


def _band_bias(rel_table):
    n_clipped = BAND_W - 1 - REL_MAX
    ext = jnp.concatenate([rel_table[:, REL_MAX - (CHUNK - 1):],
                           jnp.broadcast_to(rel_table[:, -1:], (HB, n_clipped))], axis=1)
    n = ext.shape[1]
    assert n - 1 >= BAND_W and BAND_W - 1 + CHUNK - 1 < n
    v = jnp.roll(ext[:, ::-1], -(CHUNK - 1), axis=1)
    skewed = jnp.tile(v, (1, CHUNK))[:, :CHUNK * (n - 1)].reshape(HB, CHUNK, n - 1)
    return skewed[:, :, :BAND_W].astype(F32)


def _band_bias_grouped(bias):
    g_n = BAND_GROUP
    tabs = [jnp.pad(bias, ((0, 0), (0, 0), (g * CHUNK, (g_n - 1 - g) * CHUNK)), constant_values=-jnp.inf)
            for g in range(g_n)]
    return jnp.stack(tabs, axis=1).reshape(HB * g_n * CHUNK, (g_n + BAND_CHUNKS) * CHUNK)


def _layer_weights(l, w_in, b_if, conv_w, conv_b, rel_table, c_norm_g, w_out, ln1_g, ln1_b,
                   router_g_w, router_g_b, router_e_w, router_e_b, exp_w_gate, exp_w_up, exp_w_down,
                   ln2_g, ln2_b):
    pad_cols = lambda a: jnp.pad(a, ((0, 0), (0, LANES - a.shape[1])))
    bias = _band_bias(rel_table[l])
    return dict(
        w_in=jnp.concatenate([w_in[l, :, :Z_MAIN], pad_cols(w_in[l, :, Z_MAIN:])], axis=1).astype(BF16),
        w_in_f32=jnp.concatenate([w_in[l, :, :Z_MAIN], pad_cols(w_in[l, :, Z_MAIN:])], axis=1),
        bif=pad_cols(b_if[l][None, :]),
        conv_w=conv_w[l], conv_b=conv_b[l][None, :],
        bias=bias.reshape(HB * CHUNK, BAND_W), bias_grouped=_band_bias_grouped(bias),
        norm_g=c_norm_g[l].reshape(1, DC),
        w_out=w_out[l].astype(BF16), w_out_f32=w_out[l],
        ln1_g=ln1_g[l][None, :], ln1_b=ln1_b[l][None, :],
        rw=pad_cols(jnp.concatenate([router_g_w[l], router_e_w[l]], axis=1)).astype(BF16),
        rw_f32=pad_cols(jnp.concatenate([router_g_w[l], router_e_w[l]], axis=1)),
        rb=pad_cols(jnp.concatenate([router_g_b[l], router_e_b[l]])[None, :]),
        wg=exp_w_gate, wu=exp_w_up, wd=exp_w_down, layer=l,
        ln2_g=ln2_g[l][None, :], ln2_b=ln2_b[l][None, :],
    )


def _layer(x, w, cache, *, tm, tm_moe, sb_bq, length, band_keep, kv_prev=(None, None)):
    batch, t, _ = x.shape
    n = batch * t
    tm, tm_moe = min(tm, n), min(tm_moe, n)
    x2d = x.reshape(n, D_MODEL)
    narrow = BF16 if cache is None else F32
    qa, ka, va, ka_bf, va_bf, qb, kb, vb, kb_bf, vb_bf, u, vc_bf, oc, gates = _in_proj(
        x2d, w["w_in"] if cache is None else w["w_in_f32"], tm,
        kv_stack=(batch, *kv_prev) if cache is None else None)
    per_batch = lambda a: a.reshape(batch, t, a.shape[-1])

    if cache is None:
        past = 0
        sb_k, sb_v = per_batch(ka_bf), per_batch(va_bf)
        gq = BAND_GROUP * CHUNK
        assert t % gq == 0 and BAND_CHUNKS * CHUNK % gq == 0
        band = dict(k=kb_bf, v=vb_bf, bias=w["bias_grouped"], steps=t // gq, gq=gq,
                    n_kb=BAND_CHUNKS * CHUNK // gq + 1, kbs=gq)
        cn0 = jnp.zeros((batch, HC, HDC, 2 * HDC), F32)
        m0 = jnp.zeros((batch, HC, 8, LANES), F32)
        conv0 = jnp.zeros((batch, 8, 2 * DC), F32)
    else:
        a_k, a_v, b_k, b_v, c0, n0, m_init, conv_init = cache
        past = a_k.shape[1]
        tk_pad = -(past + t) % SB_BLOCK
        sb_cat = lambda old, new: jnp.pad(
            jnp.concatenate([old.reshape(batch, past, DA).astype(narrow), per_batch(new)], axis=1),
            ((0, 0), (0, tk_pad), (0, 0)))
        sb_k, sb_v = sb_cat(a_k, ka_bf), sb_cat(a_v, va_bf)
        hist = b_k.shape[1]
        assert hist == BAND_CHUNKS * CHUNK and t == CHUNK and past % CHUNK == 0 and past >= hist
        band_cat = lambda old, new: jnp.concatenate(
            [old.reshape(batch, hist, DB).astype(narrow), per_batch(new)], axis=1).reshape(batch * BAND_W, DB)
        band = dict(k=band_cat(b_k, kb_bf), v=band_cat(b_v, vb_bf), bias=w["bias"], steps=1, gq=CHUNK,
                    n_kb=1, kbs=BAND_W)
        cn0 = jnp.concatenate([c0, n0[..., None], jnp.zeros((batch, HC, HDC, HDC - 1), F32)], axis=-1)
        m0 = jnp.broadcast_to(m_init[:, :, None, None], (batch, HC, 8, LANES))
        conv0 = jnp.pad(conv_init, ((0, 0), (8 - (CONV_W - 1), 0), (0, 0)))

    a_out = _sb_attention(qa, sb_k, sb_v, batch=batch, tq=t, past=past, bq=sb_bq, out_dtype=narrow)
    b_out = _band_attention(qb, band["k"], band["v"], band["bias"], batch=batch, steps=band["steps"],
                            gq=band["gq"], n_kb=band["n_kb"], kbs=band["kbs"], out_dtype=narrow)
    c_out, cn1, m1 = _mlstm(u, vc_bf, oc, gates, w["conv_w"], w["conv_b"], w["bif"], w["norm_g"],
                            cn0, m0, conv0, batch=batch, t=t, length=length, out_dtype=narrow)
    x1 = _out_proj(x2d, a_out, b_out, c_out, w["w_out"] if cache is None else w["w_out_f32"],
                   w["ln1_g"], w["ln1_b"], tm)
    x2 = _moe(x1, w["rw"] if cache is None else w["rw_f32"], w["rb"], w["wg"], w["wu"], w["wd"],
              w["ln2_g"], w["ln2_b"], tm_moe, w["layer"])
    heads = (lambda a: a) if cache is None else (lambda a: a.reshape(batch, t, HA, HDA))
    state = (heads(ka), heads(va),
             per_batch(kb)[:, t - band_keep:].reshape(batch, band_keep, HB, HDB),
             per_batch(vb)[:, t - band_keep:].reshape(batch, band_keep, HB, HDB),
             cn1[..., :HDC], cn1[..., HDC], m1[:, :, 0, 0], per_batch(u)[:, t - (CONV_W - 1):])
    return x2.reshape(batch, t, D_MODEL), state


def kernel(x_prompt, x_sample, cache_a_k, cache_a_v, cache_b_k, cache_b_v, state_c_C, state_c_n,
           state_c_m, state_c_conv, w_in, b_if, conv_w, conv_b, rel_table, c_norm_g, w_out, ln1_g, ln1_b,
           router_g_w, router_g_b, router_e_w, router_e_b, exp_w_gate, exp_w_up, exp_w_down, ln2_g, ln2_b):
    band_keep = cache_b_k.shape[2]
    xp, xs = x_prompt, x_sample
    prompt_states, sample_states = [], []
    kv_stack = (None, None)
    experts_bf16 = tuple(a.astype(BF16) for a in (exp_w_gate, exp_w_up, exp_w_down))
    for l in range(DEPTH):
        w = _layer_weights(l, w_in, b_if, conv_w, conv_b, rel_table, c_norm_g, w_out, ln1_g, ln1_b,
                           router_g_w, router_g_b, router_e_w, router_e_b, *experts_bf16, ln2_g, ln2_b)
        xp, sp = _layer(xp, w, None, tm=512, tm_moe=1024, sb_bq=2 * SB_BLOCK, length=2 * CHUNK,
                        band_keep=min(band_keep, xp.shape[1]), kv_prev=kv_stack)
        kv_stack = sp[:2]
        prompt_states.append(sp[2:])
        cache_l = (cache_a_k[l], cache_a_v[l], cache_b_k[l], cache_b_v[l],
                   state_c_C[l], state_c_n[l], state_c_m[l], state_c_conv[l])
        xs, ss = _layer(xs, w, cache_l, tm=512, tm_moe=512, sb_bq=CHUNK, length=CHUNK,
                        band_keep=xs.shape[1])
        sample_states.append(ss)
    p_kv = [a.reshape(DEPTH, a.shape[1], HA, HDA, a.shape[3]).transpose(0, 1, 4, 2, 3) for a in kv_stack]
    p = [jnp.stack(s) for s in zip(*prompt_states)]
    s = [jnp.stack(s) for s in zip(*sample_states)]
    return (xp, xs, *p_kv, *p, *s)
```

```python
import functools

import jax
import jax.numpy as jnp
from jax import lax
from jax.experimental import pallas as pl
from jax.experimental.pallas import tpu as pltpu

F32 = jnp.float32
BF16 = jnp.bfloat16

D_MODEL = 1024
DEPTH = 2
CHUNK = 64
HA, HDA = 4, 64
HB, HDB = 4, 64
HC, HDC = 4, 128
DA, DB, DC = HA * HDA, HB * HDB, HC * HDC
BAND_CHUNKS = 8
BAND_W = (BAND_CHUNKS + 1) * CHUNK
REL_MAX = 128
CONV_W = 4
N_GROUPS = 4
E_PER_GROUP = 4
N_EXPERTS = N_GROUPS * E_PER_GROUP
D_EXPERT = 512
ALPHA = (2 * DEPTH) ** 0.25
LN_EPS = 1e-5

LANES = 128
Z_MAIN = 3 * DA + 3 * DB + 4 * DC
Z_COLS = Z_MAIN + LANES
SB_BLOCK = 128
SB_STAY_CUTOFF = 104.0
BAND_GROUP = 4
MOE_BLOCK = 128
MOE_EXPERTS_PER_STEP = 2
VMEM_LIMIT = 48 * 1024 * 1024


def _cparams(*sem):
    return pltpu.CompilerParams(dimension_semantics=sem, vmem_limit_bytes=VMEM_LIMIT)


def _split3(x):
    hi = x.astype(BF16)
    r1 = x - hi.astype(F32)
    mid = r1.astype(BF16)
    lo = (r1 - mid.astype(F32)).astype(BF16)
    return hi, mid, lo


def _dot(a, b):
    return jnp.dot(a, b, preferred_element_type=F32)


def _dot_nt(a, b):
    return lax.dot_general(a, b, (((1,), (1,)), ((), ())), preferred_element_type=F32)


def _dot_tn(a, b):
    return lax.dot_general(a, b, (((0,), (0,)), ((), ())), preferred_element_type=F32)


def _mm(a, b, kind="nn"):
    f = {"nn": _dot, "nt": _dot_nt, "tn": _dot_tn}[kind]
    if a.dtype == BF16 and b.dtype == BF16:
        return f(a, b)
    a_hi = a.astype(BF16)
    a_lo = (a - a_hi.astype(F32)).astype(BF16)
    b_hi = b.astype(BF16)
    b_lo = (b - b_hi.astype(F32)).astype(BF16)
    return f(a_hi, b_hi) + f(a_hi, b_lo) + f(a_lo, b_hi)


def _log_sigmoid_pair(z):
    t = jnp.log1p(jnp.exp(-jnp.abs(z)))
    return -(jnp.maximum(-z, 0.0) + t), -(jnp.maximum(z, 0.0) + t)


def _stack_heads(q, n_heads, width):
    lane = lax.broadcasted_iota(jnp.int32, q.shape, 1)
    zero = jnp.zeros_like(q)
    return jnp.concatenate(
        [jnp.where((lane >= h * width) & (lane < (h + 1) * width), q, zero) for h in range(n_heads)], axis=0)


def _unstack_heads(acc, n_heads, width):
    rows = acc.shape[0] // n_heads
    lane = lax.broadcasted_iota(jnp.int32, (rows, n_heads * width), 1)
    out = acc[0:rows]
    for h in range(1, n_heads):
        out = jnp.where(lane >= h * width, acc[h * rows:(h + 1) * rows], out)
    return out


def _in_proj_kernel(x_ref, w_ref, *refs, n_stacked):
    if n_stacked:
        prev_k, prev_v, *refs = refs
    qa, ka, va, kab, vab, qb, kb, vb, kbb, vbb, u, vcb, oc, g = refs
    xb = x_ref[...].astype(w_ref.dtype)

    def proj(lo, width):
        return _mm(xb, w_ref[:, lo:lo + width])

    qa[...] = (proj(0, DA) * (HDA ** -0.5)).astype(qa.dtype)
    for f32_ref, bf_ref, lo in ((ka, kab, DA), (va, vab, 2 * DA)):
        t = proj(lo, DA)
        bf_ref[...] = t.astype(bf_ref.dtype)
        if n_stacked is None:
            f32_ref[...] = t
        else:
            f32_ref[n_stacked] = t.T
    if n_stacked:
        ka[0:n_stacked] = prev_k[...]
        va[0:n_stacked] = prev_v[...]
    for f32_ref, bf_ref, lo in ((kb, kbb, 3 * DA + DB), (vb, vbb, 3 * DA + 2 * DB)):
        t = proj(lo, DA)
        f32_ref[...] = t
        bf_ref[...] = t.astype(bf_ref.dtype)
    qb[...] = (proj(3 * DA, DB) * (HDB ** -0.5)).astype(qb.dtype)
    base = 3 * DA + 3 * DB
    for c in range(2):
        u[:, c * DC:(c + 1) * DC] = proj(base + c * DC, DC)
    vcb[...] = proj(base + 2 * DC, DC).astype(vcb.dtype)
    oc[...] = proj(base + 3 * DC, DC)
    g[...] = proj(Z_MAIN, LANES)


def _in_proj(x2d, w_bf16, tm, kv_stack=None):
    n = x2d.shape[0]
    narrow = w_bf16.dtype
    widths_dtypes = [(DA, narrow), (DA, F32), (DA, F32), (DA, narrow), (DA, narrow),
                     (DB, narrow), (DB, F32), (DB, F32), (DB, narrow), (DB, narrow),
                     (2 * DC, F32), (DC, narrow), (DC, F32), (LANES, F32)]
    out_specs = [pl.BlockSpec((tm, w), lambda i: (i, 0)) for w, _ in widths_dtypes]
    out_shape = [jax.ShapeDtypeStruct((n, w), dt) for w, dt in widths_dtypes]
    in_specs = [pl.BlockSpec((tm, D_MODEL), lambda i: (i, 0)), pl.BlockSpec((D_MODEL, Z_COLS), lambda i: (0, 0))]
    operands = [x2d, w_bf16]
    n_stacked = None
    if kv_stack is not None:
        batch, prev_k, prev_v = kv_stack
        tiles = n // batch // tm
        n_stacked = 0 if prev_k is None else prev_k.shape[0]
        stack_spec = lambda layers: pl.BlockSpec((layers, None, DA, tm), lambda i: (0, i // tiles, 0, i % tiles))
        for idx in (1, 2):
            out_specs[idx] = stack_spec(n_stacked + 1)
            out_shape[idx] = jax.ShapeDtypeStruct((n_stacked + 1, batch, DA, tiles * tm), F32)
        if n_stacked:
            in_specs += [stack_spec(n_stacked)] * 2
            operands += [prev_k, prev_v]
    return pl.pallas_call(
        functools.partial(_in_proj_kernel, n_stacked=n_stacked),
        grid=(n // tm,),
        in_specs=in_specs,
        out_specs=out_specs,
        out_shape=out_shape,
        compiler_params=_cparams("parallel"),
        name="in_proj",
    )(*operands)


def _sb_kernel(q_ref, k_ref, v_ref, o_ref, carry_ref, acc_ref, *, past, bq, bk, n_masked):
    i = pl.program_id(1)
    qs = _stack_heads(q_ref[...], HA, HDA)
    jj = lax.broadcasted_iota(jnp.int32, (bk, 2 * bk), 0)
    ss = lax.broadcasted_iota(jnp.int32, (bk, 2 * bk), 1)
    csum = jnp.where((ss >= bk) | (jj > ss), 1.0, 0.0).astype(BF16)
    csum2 = jnp.concatenate([csum, csum], axis=0)
    kb_last = (past + (i + 1) * bq - 2) // bk

    def block(kb, masked, first):
        start = pl.multiple_of(kb * bk, bk)
        s = _mm(qs, k_ref[pl.ds(start, bk), :], "nt")
        cost = jnp.maximum(s, 0.0) + jnp.log(1.0 + jnp.exp(-jnp.abs(s)))
        if masked:
            row = lax.broadcasted_iota(jnp.int32, (HA * bq, bk), 0) & (bq - 1)
            mask = (start + lax.broadcasted_iota(jnp.int32, (HA * bq, bk), 1)) < past + i * bq + row
            stay = jnp.where(mask, cost, 0.0)
        else:
            stay = cost
        hi = stay.astype(BF16)
        lo = (stay - hi.astype(F32)).astype(BF16)
        cs = _dot(jnp.concatenate([hi, lo], axis=1), csum2)
        before = cs[:, :bk] if first else carry_ref[...] + cs[:, :bk]
        w = jnp.exp(s - cost - before)
        if masked:
            w = jnp.where(mask, w, 0.0)
        pv = _mm(w.astype(v_ref.dtype), v_ref[pl.ds(start, bk), :])
        if first:
            carry = cs[:, bk:]
            acc_ref[...] = pv
        else:
            carry = carry_ref[...] + cs[:, bk:]
            acc_ref[...] += pv
        carry_ref[...] = carry
        return (jnp.min(carry) > SB_STAY_CUTOFF).astype(jnp.int32)

    done0 = block(kb_last, True, True)
    for d in range(1, n_masked):
        done0 = block(kb_last - d, True, False)

    def cond(state):
        kb, done = state
        return jnp.logical_and(kb >= 0, done == 0)

    def body(state):
        kb, _ = state
        return kb - 1, block(kb, False, False)

    lax.while_loop(cond, body, (kb_last - n_masked, done0))
    o_ref[...] = _unstack_heads(acc_ref[...], HA, HDA).astype(o_ref.dtype)


def _sb_attention(q_bf16, k_bf16, v_bf16, *, batch, tq, past, bq, out_dtype):
    bk = SB_BLOCK
    nq = tq // bq
    tk = k_bf16.shape[1]
    assert tk % bk == 0 and (past + tq - 2) // bk < tk // bk
    assert past % bk == 0 and (bq % bk == 0 or (nq == 1 and bq <= bk))
    n_masked = max(bq // bk, 1)
    return pl.pallas_call(
        functools.partial(_sb_kernel, past=past, bq=bq, bk=bk, n_masked=n_masked),
        grid=(batch, nq),
        in_specs=[pl.BlockSpec((bq, DA), lambda b, i: (b * nq + i, 0)),
                  pl.BlockSpec((None, tk, DA), lambda b, i: (b, 0, 0)),
                  pl.BlockSpec((None, tk, DA), lambda b, i: (b, 0, 0))],
        out_specs=pl.BlockSpec((bq, DA), lambda b, i: (b * nq + i, 0)),
        out_shape=jax.ShapeDtypeStruct((batch * tq, DA), out_dtype),
        scratch_shapes=[pltpu.VMEM((HA * bq, bk), F32), pltpu.VMEM((HA * bq, DA), F32)],
        compiler_params=_cparams("parallel", "arbitrary"),
        name="sb_attention",
    )(q_bf16, k_bf16, v_bf16)


def _band_kernel(q_ref, *refs, n_kb, rows_per_pass):
    k_refs, v_refs = refs[:n_kb], refs[n_kb:2 * n_kb]
    bias_ref, o_ref, s_ref, p_ref = refs[2 * n_kb:]
    j = pl.program_id(1)
    qs = _stack_heads(q_ref[...], HB, HDB)
    kbs = k_refs[0].shape[0]
    for i in range(n_kb):
        s = _mm(qs, k_refs[i][...], "nt") + bias_ref[:, i * kbs:(i + 1) * kbs]
        if i < n_kb - 1:
            s = s + jnp.where(j + i < n_kb - 1, -jnp.inf, 0.0)
        s_ref[:, i * kbs:(i + 1) * kbs] = s
    half = s_ref.shape[0] // 2
    accs = []
    for h0 in (0, half):
        for r in range(h0, h0 + half, rows_per_pass):
            s = s_ref[r:r + rows_per_pass, :]
            e = jnp.exp(s - jnp.max(s, axis=-1, keepdims=True))
            p_ref[r:r + rows_per_pass, :] = (e * (1.0 / jnp.sum(e, axis=-1, keepdims=True))).astype(p_ref.dtype)
        acc = _mm(p_ref[h0:h0 + half, 0:kbs], v_refs[0][...])
        for i in range(1, n_kb):
            acc = acc + _mm(p_ref[h0:h0 + half, i * kbs:(i + 1) * kbs], v_refs[i][...])
        accs.append(acc)
    o_ref[...] = _unstack_heads(jnp.concatenate(accs, axis=0), HB, HDB).astype(o_ref.dtype)


def _band_attention(q_bf16, k_bf16, v_bf16, bias, *, batch, steps, gq, n_kb, kbs, out_dtype):
    blocks_per_batch = k_bf16.shape[0] // (batch * kbs)
    kv_specs = [pl.BlockSpec((kbs, DB), functools.partial(
        lambda b, j, i: (b * blocks_per_batch + jnp.maximum(j + i - (n_kb - 1), 0), 0), i=i)) for i in range(n_kb)]
    return pl.pallas_call(
        functools.partial(_band_kernel, n_kb=n_kb, rows_per_pass=min(128, HB * gq)),
        grid=(batch, steps),
        in_specs=[pl.BlockSpec((gq, DB), lambda b, j: (b * steps + j, 0))] + kv_specs + kv_specs
        + [pl.BlockSpec((HB * gq, n_kb * kbs), lambda b, j: (0, 0))],
        out_specs=pl.BlockSpec((gq, DB), lambda b, j: (b * steps + j, 0)),
        out_shape=jax.ShapeDtypeStruct((batch * steps * gq, DB), out_dtype),
        scratch_shapes=[pltpu.VMEM((HB * gq, n_kb * kbs), F32), pltpu.VMEM((HB * gq, n_kb * kbs), v_bf16.dtype)],
        compiler_params=_cparams("parallel", "arbitrary"),
        name="band_attention",
    )(q_bf16, *([k_bf16] * n_kb), *([v_bf16] * n_kb), bias)


def _mlstm_kernel(u_ref, vc_ref, oc_ref, g_ref, convw_ref, convb_ref, bif_ref, ng_ref,
                  cn0_ref, m0_ref, conv0_ref, o_ref, cn_ref, m_ref, ubuf, *, length):
    L = length
    c = pl.program_id(1)

    @pl.when(c == 0)
    def _():
        cn_ref[...] = cn0_ref[...]
        m_ref[...] = m0_ref[...]
        ubuf[0:8, :] = conv0_ref[...]

    ubuf[8:8 + L, :] = u_ref[...]
    y = convb_ref[...]
    for w in range(CONV_W):
        y = y + ubuf[8 - (CONV_W - 1) + w:8 - (CONV_W - 1) + w + L, :] * convw_ref[w:w + 1, :]
    tail = ubuf[L:L + 8, :]
    ubuf[0:8, :] = tail
    qk = y * jax.nn.sigmoid(y)

    g = g_ref[...] + bif_ref[...]
    lf_all, _ = _log_sigmoid_pair(g)
    r_i = lax.broadcasted_iota(jnp.int32, (L, L), 0)
    c_i = lax.broadcasted_iota(jnp.int32, (L, L), 1)
    causal = c_i <= r_i
    tril = jnp.where(causal, 1.0, 0.0).astype(BF16)
    hi, mid, lo = _split3(lf_all)
    b_all = _dot(tril, hi) + _dot(tril, mid) + _dot(tril, lo)
    lane = lax.broadcasted_iota(jnp.int32, (L, LANES), 1)
    mixed = jnp.where(lane < HC, g, b_all)
    eye = jnp.where(lax.broadcasted_iota(jnp.int32, (16, LANES), 0)
                    == lax.broadcasted_iota(jnp.int32, (16, LANES), 1), 1.0, 0.0).astype(BF16)
    hi, mid, lo = _split3(mixed)
    rows = _dot_nt(eye, hi) + _dot_nt(eye, mid) + _dot_nt(eye, lo)
    narrow = vc_ref.dtype
    ones_col = jnp.where(lax.broadcasted_iota(jnp.int32, (L, HDC), 1) == 0, 1.0, 0.0).astype(narrow)

    for h in range(HC):
        li_row = rows[h:h + 1, :]
        b_row = rows[HC + h:HC + h + 1, :]
        li_col = g[:, h:h + 1]
        b_col = b_all[:, HC + h:HC + h + 1]
        dmat = jnp.where(causal, b_col - b_row + li_row, -jnp.inf)
        m_prev = m_ref[h][0:1, 0:1]
        inter = b_col + m_prev
        m_t = jnp.maximum(inter, jnp.max(dmat, axis=-1, keepdims=True))
        w_intra = jnp.exp(dmat - m_t)
        w_inter = jnp.exp(inter - m_t)
        qh = qk[:, h * HDC:(h + 1) * HDC].astype(narrow)
        kf = qk[:, DC + h * HDC:DC + (h + 1) * HDC] * (HDC ** -0.5)
        vaug = jnp.concatenate([vc_ref[:, h * HDC:(h + 1) * HDC], ones_col], axis=1)
        s = _mm(qh, kf.astype(narrow), "nt") * w_intra
        cn = cn_ref[h]
        qcn = _mm(qh, cn.astype(narrow))
        sv = _mm(s.astype(narrow), vaug[:, :HDC])
        num = w_inter * qcn[:, :HDC] + sv
        den = w_inter * qcn[:, HDC:HDC + 1] + jnp.sum(s, axis=-1, keepdims=True)
        hh = num / jnp.maximum(jnp.abs(den), jnp.exp(-m_t))
        m_new = m_t[L - 1:L, :]
        decay = jnp.exp(inter[L - 1:L, :] - m_new)
        w_k = jnp.exp(b_col[L - 1:L, :] - b_col + li_col - m_new)
        cn_ref[h] = decay * cn + _mm((kf * w_k).astype(narrow), vaug, "tn")
        m_ref[h] = jnp.broadcast_to(m_new, (8, LANES))
        mu = jnp.mean(hh, axis=-1, keepdims=True)
        dev = hh - mu
        var = jnp.mean(dev * dev, axis=-1, keepdims=True)
        hn = dev * lax.rsqrt(var + LN_EPS) * ng_ref[:, h * HDC:(h + 1) * HDC]
        gate = jax.nn.sigmoid(oc_ref[:, h * HDC:(h + 1) * HDC])
        o_ref[:, h * HDC:(h + 1) * HDC] = (gate * hn).astype(o_ref.dtype)


def _mlstm(u, vc_bf16, oc, gates, conv_w, conv_b, bif_pad, norm_g, cn0, m0, conv0, *, batch, t, length,
           out_dtype):
    nc = t // length
    row = lambda w: pl.BlockSpec((length, w), lambda b, c: (b * nc + c, 0))
    const = lambda shape: pl.BlockSpec(shape, lambda b, c: (0,) * len(shape))
    per_batch = lambda shape: pl.BlockSpec((None,) + shape, lambda b, c: (b,) + (0,) * len(shape))
    return pl.pallas_call(
        functools.partial(_mlstm_kernel, length=length),
        grid=(batch, nc),
        in_specs=[row(2 * DC), row(DC), row(DC), row(LANES),
                  const((CONV_W, 2 * DC)), const((1, 2 * DC)), const((1, LANES)), const((1, DC)),
                  per_batch((HC, HDC, 2 * HDC)), per_batch((HC, 8, LANES)), per_batch((8, 2 * DC))],
        out_specs=[row(DC), per_batch((HC, HDC, 2 * HDC)), per_batch((HC, 8, LANES))],
        out_shape=[jax.ShapeDtypeStruct((batch * t, DC), out_dtype),
                   jax.ShapeDtypeStruct((batch, HC, HDC, 2 * HDC), F32),
                   jax.ShapeDtypeStruct((batch, HC, 8, LANES), F32)],
        scratch_shapes=[pltpu.VMEM((8 + length, 2 * DC), F32)],
        compiler_params=_cparams("parallel", "arbitrary"),
        name="mlstm",
    )(u, vc_bf16, oc, gates, conv_w, conv_b, bif_pad, norm_g, cn0, m0, conv0)


def _layer_norm(y, g, b):
    mu = jnp.mean(y, axis=-1, keepdims=True)
    dev = y - mu
    var = jnp.mean(dev * dev, axis=-1, keepdims=True)
    return dev * lax.rsqrt(var + LN_EPS) * g + b


def _out_proj_kernel(x_ref, a_ref, b_ref, c_ref, w_ref, g_ref, beta_ref, o_ref):
    mix = (_mm(a_ref[...], w_ref[0:DA, :]) + _mm(b_ref[...], w_ref[DA:DA + DB, :])
           + _mm(c_ref[...], w_ref[DA + DB:, :]))
    o_ref[...] = _layer_norm(ALPHA * x_ref[...] + mix, g_ref[...], beta_ref[...])


def _out_proj(x2d, a, b, c, w, g, beta, tm):
    assert a.dtype == b.dtype == c.dtype == w.dtype
    n = x2d.shape[0]
    row = lambda w: pl.BlockSpec((tm, w), lambda i: (i, 0))
    const = lambda shape: pl.BlockSpec(shape, lambda i: (0, 0))
    return pl.pallas_call(
        _out_proj_kernel,
        grid=(n // tm,),
        in_specs=[row(D_MODEL), row(DA), row(DB), row(DC), const((D_MODEL, D_MODEL)),
                  const((1, D_MODEL)), const((1, D_MODEL))],
        out_specs=row(D_MODEL),
        out_shape=jax.ShapeDtypeStruct((n, D_MODEL), F32),
        compiler_params=_cparams("parallel"),
        name="out_proj_ln",
    )(x2d, a, b, c, w, g, beta)


def _route(logits):
    lane = lax.broadcasted_iota(jnp.int32, logits.shape, 1)
    big = jnp.int32(LANES)
    is_g = lane < N_GROUPS
    lg = jnp.where(is_g, logits, -jnp.inf)
    g_max = jnp.max(lg, axis=-1, keepdims=True)
    g_sel = jnp.min(jnp.where(lg == g_max, lane, big), axis=-1, keepdims=True)
    p_g = 1.0 / jnp.sum(jnp.where(is_g, jnp.exp(logits - g_max), 0.0), axis=-1, keepdims=True)
    e_lo = N_GROUPS + g_sel * E_PER_GROUP
    in_group = (lane >= e_lo) & (lane < e_lo + E_PER_GROUP)
    le = jnp.where(in_group, logits, -jnp.inf)
    v1 = jnp.max(le, axis=-1, keepdims=True)
    i1 = jnp.min(jnp.where(le == v1, lane, big), axis=-1, keepdims=True)
    le2 = jnp.where(lane == i1, -jnp.inf, le)
    v2 = jnp.max(le2, axis=-1, keepdims=True)
    i2 = jnp.min(jnp.where(le2 == v2, lane, big), axis=-1, keepdims=True)
    e2 = jnp.exp(v2 - v1)
    tot = 1.0 + e2
    comb = jnp.where(lane == i1, p_g * (1.0 / tot), 0.0) + jnp.where(lane == i2, p_g * (e2 / tot), 0.0)
    return comb, g_sel


def _moe_kernel(x_ref, rw_ref, rb_ref, wg_ref, wu_ref, wd_ref, g_ref, beta_ref, o_ref,
                xs_ref, combs_ref, acc_ref, pos_ref, tab_ref):
    step = pl.program_id(1)
    tm = x_ref.shape[0]
    n_sorted = xs_ref.shape[0]
    blk = MOE_BLOCK

    @pl.when(step == 0)
    def _():
        xb = x_ref[...].astype(BF16)
        logits = _mm(x_ref[...].astype(rw_ref.dtype), rw_ref[...])
        comb, g_sel = _route(logits + rb_ref[...])
        lane = lax.broadcasted_iota(jnp.int32, (tm, LANES), 1)
        onehot = jnp.where(lane == g_sel, 1.0, 0.0)
        onehot_bf = onehot.astype(BF16)
        col = lax.broadcasted_iota(jnp.int32, (blk, tm), 1)
        row = lax.broadcasted_iota(jnp.int32, (blk, tm), 0)
        cum = jnp.concatenate(
            [_dot(jnp.where(col <= row + r, 1.0, 0.0).astype(BF16), onehot_bf) for r in range(0, tm, blk)], axis=0)
        count = cum[tm - 1:tm, :]
        padded = jnp.floor((count + (blk - 1)) * (1.0 / blk)) * blk
        lane1 = lax.broadcasted_iota(jnp.int32, (1, LANES), 1)
        offsets = jnp.zeros((1, LANES), F32)
        for grp in range(N_GROUPS):
            off = jnp.sum(jnp.where(lane1 < grp, padded, 0.0), axis=-1, keepdims=True)
            n_blocks = jnp.sum(jnp.where(lane1 == grp, padded, 0.0), axis=-1, keepdims=True) * (1.0 / blk)
            offsets = jnp.where(lane1 == grp, off, offsets)
            tab_ref[grp] = off.astype(jnp.int32)[0, 0]
            tab_ref[N_GROUPS + grp] = n_blocks.astype(jnp.int32)[0, 0]
        pos = jnp.sum(onehot * (offsets + cum), axis=-1, keepdims=True) - 1.0
        pos_ref[...] = pos.astype(jnp.int32)
        pos_hi = jnp.floor(pos * (1.0 / 256.0))
        digits = jnp.where(lane == 0, pos_hi, jnp.where(lane == 1, pos - 256.0 * pos_hi, 0.0)).astype(BF16)
        eye = jnp.where(lax.broadcasted_iota(jnp.int32, (16, LANES), 0)
                        == lax.broadcasted_iota(jnp.int32, (16, LANES), 1), 1.0, 0.0).astype(BF16)
        dig_rows = _dot_nt(eye, digits)
        pos_row = (dig_rows[0:1, :] * 256.0 + dig_rows[1:2, :]).astype(jnp.int32)
        c_hi = comb.astype(BF16)
        c_lo = (comb - c_hi.astype(F32)).astype(BF16)
        payload = jnp.concatenate([xb, c_hi, c_lo], axis=1)
        for r in range(0, n_sorted, blk):
            perm = jnp.where(row == pos_row - r, 1.0, 0.0).astype(BF16)
            moved = _dot(perm, payload)
            xs_ref[r:r + blk, :] = moved[:, :D_MODEL].astype(BF16)
            combs_ref[r:r + blk, :] = moved[:, D_MODEL:D_MODEL + LANES] + moved[:, D_MODEL + LANES:]
        acc_ref[...] = jnp.zeros_like(acc_ref)

    grp = (step * MOE_EXPERTS_PER_STEP) // E_PER_GROUP
    start = tab_ref[grp]
    n_blocks = tab_ref[N_GROUPS + grp]

    def expert_rows(r0, rows):
        xsb = xs_ref[pl.ds(r0, rows), :]
        combs = combs_ref[pl.ds(r0, rows), :]
        acc = acc_ref[pl.ds(r0, rows), :]
        lane_b = lax.broadcasted_iota(jnp.int32, (rows, LANES), 1)
        for j in range(MOE_EXPERTS_PER_STEP):
            gate = _dot(xsb, wg_ref[j])
            h = gate * jax.nn.sigmoid(gate) * _dot(xsb, wu_ref[j])
            y = _dot(h.astype(BF16), wd_ref[j])
            lane_e = N_GROUPS + step * MOE_EXPERTS_PER_STEP + j
            acc = acc + jnp.sum(jnp.where(lane_b == lane_e, combs, 0.0), axis=-1, keepdims=True) * y
        acc_ref[pl.ds(r0, rows), :] = acc

    def block_pair(b, carry):
        expert_rows(pl.multiple_of(start + b * (2 * blk), blk), 2 * blk)
        return carry

    lax.fori_loop(0, n_blocks // 2, block_pair, 0)

    @pl.when(n_blocks % 2 == 1)
    def _():
        expert_rows(pl.multiple_of(start + (n_blocks - 1) * blk, blk), blk)

    @pl.when(step == N_EXPERTS // MOE_EXPERTS_PER_STEP - 1)
    def _():
        for r in range(0, n_sorted, blk):
            xs_ref[r:r + blk, :] = acc_ref[r:r + blk, :].astype(BF16)
        lane_s = lax.broadcasted_iota(jnp.int32, (blk, n_sorted), 1)
        for r in range(0, tm, blk):
            back = jnp.where(lane_s == pos_ref[r:r + blk, :], 1.0, 0.0).astype(BF16)
            y = _dot(back, xs_ref[...])
            o_ref[r:r + blk, :] = _layer_norm(ALPHA * x_ref[r:r + blk, :] + y, g_ref[...], beta_ref[...])


def _moe(x2d, rw, rb, wg, wu, wd, g, beta, tm, layer):
    n = x2d.shape[0]
    n_sorted = tm + N_GROUPS * MOE_BLOCK
    per_step = MOE_EXPERTS_PER_STEP
    assert tm % MOE_BLOCK == 0 and n_sorted < 256 * 256 and E_PER_GROUP % per_step == 0
    const = lambda shape: pl.BlockSpec(shape, lambda i, e: (0, 0))
    return pl.pallas_call(
        _moe_kernel,
        grid=(n // tm, N_EXPERTS // per_step),
        in_specs=[pl.BlockSpec((tm, D_MODEL), lambda i, e: (i, 0)),
                  const((D_MODEL, LANES)), const((1, LANES)),
                  pl.BlockSpec((None, per_step, D_MODEL, D_EXPERT), lambda i, e: (layer, e, 0, 0)),
                  pl.BlockSpec((None, per_step, D_MODEL, D_EXPERT), lambda i, e: (layer, e, 0, 0)),
                  pl.BlockSpec((None, per_step, D_EXPERT, D_MODEL), lambda i, e: (layer, e, 0, 0)),
                  const((1, D_MODEL)), const((1, D_MODEL))],
        out_specs=pl.BlockSpec((tm, D_MODEL), lambda i, e: (i, 0)),
        out_shape=jax.ShapeDtypeStruct((n, D_MODEL), F32),
        scratch_shapes=[pltpu.VMEM((n_sorted, D_MODEL), BF16),
                        pltpu.VMEM((n_sorted, LANES), F32), pltpu.VMEM((n_sorted, D_MODEL), F32),
                        pltpu.VMEM((tm, 1), jnp.int32), pltpu.SMEM((2 * N_GROUPS,), jnp.int32)],
        compiler_params=_cparams("parallel", "arbitrary"),
        name="moe_ln",
    )(x2d, rw, rb, wg, wu, wd, g, beta)


def _band_bias(rel_table):
    n_clipped = BAND_W - 1 - REL_MAX
    ext = jnp.concatenate([rel_table[:, REL_MAX - (CHUNK - 1):],
                           jnp.broadcast_to(rel_table[:, -1:], (HB, n_clipped))], axis=1)
    n = ext.shape[1]
    assert n - 1 >= BAND_W and BAND_W - 1 + CHUNK - 1 < n
    v = jnp.roll(ext[:, ::-1], -(CHUNK - 1), axis=1)
    skewed = jnp.tile(v, (1, CHUNK))[:, :CHUNK * (n - 1)].reshape(HB, CHUNK, n - 1)
    return skewed[:, :, :BAND_W].astype(F32)


def _band_bias_grouped(bias):
    g_n = BAND_GROUP
    tabs = [jnp.pad(bias, ((0, 0), (0, 0), (g * CHUNK, (g_n - 1 - g) * CHUNK)), constant_values=-jnp.inf)
            for g in range(g_n)]
    return jnp.stack(tabs, axis=1).reshape(HB * g_n * CHUNK, (g_n + BAND_CHUNKS) * CHUNK)


def _layer_weights(l, w_in, b_if, conv_w, conv_b, rel_table, c_norm_g, w_out, ln1_g, ln1_b,
                   router_g_w, router_g_b, router_e_w, router_e_b, exp_w_gate, exp_w_up, exp_w_down,
                   ln2_g, ln2_b):
    pad_cols = lambda a: jnp.pad(a, ((0, 0), (0, LANES - a.shape[1])))
    bias = _band_bias(rel_table[l])
    return dict(
        w_in=jnp.concatenate([w_in[l, :, :Z_MAIN], pad_cols(w_in[l, :, Z_MAIN:])], axis=1).astype(BF16),
        w_in_f32=jnp.concatenate([w_in[l, :, :Z_MAIN], pad_cols(w_in[l, :, Z_MAIN:])], axis=1),
        bif=pad_cols(b_if[l][None, :]),
        conv_w=conv_w[l], conv_b=conv_b[l][None, :],
        bias=bias.reshape(HB * CHUNK, BAND_W), bias_grouped=_band_bias_grouped(bias),
        norm_g=c_norm_g[l].reshape(1, DC),
        w_out=w_out[l].astype(BF16), w_out_f32=w_out[l],
        ln1_g=ln1_g[l][None, :], ln1_b=ln1_b[l][None, :],
        rw=pad_cols(jnp.concatenate([router_g_w[l], router_e_w[l]], axis=1)).astype(BF16),
        rw_f32=pad_cols(jnp.concatenate([router_g_w[l], router_e_w[l]], axis=1)),
        rb=pad_cols(jnp.concatenate([router_g_b[l], router_e_b[l]])[None, :]),
        wg=exp_w_gate, wu=exp_w_up, wd=exp_w_down, layer=l,
        ln2_g=ln2_g[l][None, :], ln2_b=ln2_b[l][None, :],
    )


def _layer(x, w, cache, *, tm, tm_moe, sb_bq, length, band_keep, kv_prev=(None, None), split_products=False):
    batch, t, _ = x.shape
    n = batch * t
    tm, tm_moe = min(tm, n), min(tm_moe, n)
    x2d = x.reshape(n, D_MODEL)
    narrow = F32 if split_products else BF16
    pick = (lambda name: w[name + "_f32"]) if split_products else (lambda name: w[name])
    qa, ka, va, ka_bf, va_bf, qb, kb, vb, kb_bf, vb_bf, u, vc_bf, oc, gates = _in_proj(
        x2d, pick("w_in"), tm, kv_stack=(batch, *kv_prev) if cache is None else None)
    per_batch = lambda a: a.reshape(batch, t, a.shape[-1])

    if cache is None:
        past = 0
        sb_k, sb_v = per_batch(ka_bf), per_batch(va_bf)
        gq = BAND_GROUP * CHUNK
        assert t % gq == 0 and BAND_CHUNKS * CHUNK % gq == 0
        band = dict(k=kb_bf, v=vb_bf, bias=w["bias_grouped"], steps=t // gq, gq=gq,
                    n_kb=BAND_CHUNKS * CHUNK // gq + 1, kbs=gq)
        cn0 = jnp.zeros((batch, HC, HDC, 2 * HDC), F32)
        m0 = jnp.zeros((batch, HC, 8, LANES), F32)
        conv0 = jnp.zeros((batch, 8, 2 * DC), F32)
    else:
        a_k, a_v, b_k, b_v, c0, n0, m_init, conv_init = cache
        past = a_k.shape[1]
        tk_pad = -(past + t) % SB_BLOCK
        sb_cat = lambda old, new: jnp.pad(
            jnp.concatenate([old.reshape(batch, past, DA).astype(narrow), per_batch(new)], axis=1),
            ((0, 0), (0, tk_pad), (0, 0)))
        sb_k, sb_v = sb_cat(a_k, ka_bf), sb_cat(a_v, va_bf)
        hist = b_k.shape[1]
        assert hist == BAND_CHUNKS * CHUNK and t == CHUNK and past % CHUNK == 0 and past >= hist
        band_cat = lambda old, new: jnp.concatenate(
            [old.reshape(batch, hist, DB).astype(narrow), per_batch(new)], axis=1).reshape(batch * BAND_W, DB)
        band = dict(k=band_cat(b_k, kb_bf), v=band_cat(b_v, vb_bf), bias=w["bias"], steps=1, gq=CHUNK,
                    n_kb=1, kbs=BAND_W)
        cn0 = jnp.concatenate([c0, n0[..., None], jnp.zeros((batch, HC, HDC, HDC - 1), F32)], axis=-1)
        m0 = jnp.broadcast_to(m_init[:, :, None, None], (batch, HC, 8, LANES))
        conv0 = jnp.pad(conv_init, ((0, 0), (8 - (CONV_W - 1), 0), (0, 0)))

    a_out = _sb_attention(qa, sb_k, sb_v, batch=batch, tq=t, past=past, bq=sb_bq, out_dtype=narrow)
    b_out = _band_attention(qb, band["k"], band["v"], band["bias"], batch=batch, steps=band["steps"],
                            gq=band["gq"], n_kb=band["n_kb"], kbs=band["kbs"], out_dtype=narrow)
    c_out, cn1, m1 = _mlstm(u, vc_bf, oc, gates, w["conv_w"], w["conv_b"], w["bif"], w["norm_g"],
                            cn0, m0, conv0, batch=batch, t=t, length=length, out_dtype=narrow)
    x1 = _out_proj(x2d, a_out, b_out, c_out, pick("w_out"), w["ln1_g"], w["ln1_b"], tm)
    x2 = _moe(x1, pick("rw"), w["rb"], w["wg"], w["wu"], w["wd"], w["ln2_g"], w["ln2_b"], tm_moe, w["layer"])
    heads = (lambda a: a) if cache is None else (lambda a: a.reshape(batch, t, HA, HDA))
    state = (heads(ka), heads(va),
             per_batch(kb)[:, t - band_keep:].reshape(batch, band_keep, HB, HDB),
             per_batch(vb)[:, t - band_keep:].reshape(batch, band_keep, HB, HDB),
             cn1[..., :HDC], cn1[..., HDC], m1[:, :, 0, 0], per_batch(u)[:, t - (CONV_W - 1):])
    return x2.reshape(batch, t, D_MODEL), state


def kernel(x_prompt, x_sample, cache_a_k, cache_a_v, cache_b_k, cache_b_v, state_c_C, state_c_n,
           state_c_m, state_c_conv, w_in, b_if, conv_w, conv_b, rel_table, c_norm_g, w_out, ln1_g, ln1_b,
           router_g_w, router_g_b, router_e_w, router_e_b, exp_w_gate, exp_w_up, exp_w_down, ln2_g, ln2_b):
    band_keep = cache_b_k.shape[2]
    xp, xs = x_prompt, x_sample
    prompt_states, sample_states = [], []
    kv_stack = (None, None)
    experts_bf16 = tuple(a.astype(BF16) for a in (exp_w_gate, exp_w_up, exp_w_down))
    for l in range(DEPTH):
        w = _layer_weights(l, w_in, b_if, conv_w, conv_b, rel_table, c_norm_g, w_out, ln1_g, ln1_b,
                           router_g_w, router_g_b, router_e_w, router_e_b, *experts_bf16, ln2_g, ln2_b)
        xp, sp = _layer(xp, w, None, tm=512, tm_moe=1024, sb_bq=2 * SB_BLOCK, length=2 * CHUNK,
                        band_keep=min(band_keep, xp.shape[1]), kv_prev=kv_stack)
        kv_stack = sp[:2]
        prompt_states.append(sp[2:])
        cache_l = (cache_a_k[l], cache_a_v[l], cache_b_k[l], cache_b_v[l],
                   state_c_C[l], state_c_n[l], state_c_m[l], state_c_conv[l])
        xs, ss = _layer(xs, w, cache_l, tm=512, tm_moe=512, sb_bq=CHUNK, length=CHUNK,
                        band_keep=xs.shape[1], split_products=l < DEPTH - 1)
        sample_states.append(ss)
    p_kv = [a.reshape(DEPTH, a.shape[1], HA, HDA, a.shape[3]).transpose(0, 1, 4, 2, 3) for a in kv_stack]
    p = [jnp.stack(s) for s in zip(*prompt_states)]
    s = [jnp.stack(s) for s in zip(*sample_states)]
    return (xp, xs, *p_kv, *p, *s)
```

```python
import functools

import jax
import jax.numpy as jnp
from jax import lax
from jax.experimental import pallas as pl
from jax.experimental.pallas import tpu as pltpu

F32 = jnp.float32
BF16 = jnp.bfloat16

D_MODEL = 1024
DEPTH = 2
CHUNK = 64
HA, HDA = 4, 64
HB, HDB = 4, 64
HC, HDC = 4, 128
DA, DB, DC = HA * HDA, HB * HDB, HC * HDC
BAND_CHUNKS = 8
BAND_W = (BAND_CHUNKS + 1) * CHUNK
REL_MAX = 128
CONV_W = 4
N_GROUPS = 4
E_PER_GROUP = 4
N_EXPERTS = N_GROUPS * E_PER_GROUP
D_EXPERT = 512
ALPHA = (2 * DEPTH) ** 0.25
LN_EPS = 1e-5

LANES = 128
Z_MAIN = 3 * DA + 3 * DB + 4 * DC
Z_COLS = Z_MAIN + LANES
SB_BLOCK = 128
SB_STAY_CUTOFF = 104.0
BAND_GROUP = 4
MOE_BLOCK = 128
MOE_EXPERTS_PER_STEP = 2
VMEM_LIMIT = 48 * 1024 * 1024


def _cparams(*sem):
    return pltpu.CompilerParams(dimension_semantics=sem, vmem_limit_bytes=VMEM_LIMIT)


def _split3(x):
    hi = x.astype(BF16)
    r1 = x - hi.astype(F32)
    mid = r1.astype(BF16)
    lo = (r1 - mid.astype(F32)).astype(BF16)
    return hi, mid, lo


def _dot(a, b):
    return jnp.dot(a, b, preferred_element_type=F32)


def _dot_nt(a, b):
    return lax.dot_general(a, b, (((1,), (1,)), ((), ())), preferred_element_type=F32)


def _dot_tn(a, b):
    return lax.dot_general(a, b, (((0,), (0,)), ((), ())), preferred_element_type=F32)


def _mm(a, b, kind="nn"):
    f = {"nn": _dot, "nt": _dot_nt, "tn": _dot_tn}[kind]
    if a.dtype == BF16 and b.dtype == BF16:
        return f(a, b)
    a_hi = a.astype(BF16)
    a_lo = (a - a_hi.astype(F32)).astype(BF16)
    b_hi = b.astype(BF16)
    b_lo = (b - b_hi.astype(F32)).astype(BF16)
    return f(a_hi, b_hi) + f(a_hi, b_lo) + f(a_lo, b_hi)


def _log_sigmoid_pair(z):
    t = jnp.log1p(jnp.exp(-jnp.abs(z)))
    return -(jnp.maximum(-z, 0.0) + t), -(jnp.maximum(z, 0.0) + t)


def _stack_heads(q, n_heads, width):
    lane = lax.broadcasted_iota(jnp.int32, q.shape, 1)
    zero = jnp.zeros_like(q)
    return jnp.concatenate(
        [jnp.where((lane >= h * width) & (lane < (h + 1) * width), q, zero) for h in range(n_heads)], axis=0)


def _unstack_heads(acc, n_heads, width):
    rows = acc.shape[0] // n_heads
    lane = lax.broadcasted_iota(jnp.int32, (rows, n_heads * width), 1)
    out = acc[0:rows]
    for h in range(1, n_heads):
        out = jnp.where(lane >= h * width, acc[h * rows:(h + 1) * rows], out)
    return out


def _in_proj_kernel(x_ref, w_ref, *refs, n_stacked):
    if n_stacked:
        prev_k, prev_v, *refs = refs
    qa, ka, va, kab, vab, qb, kb, vb, kbb, vbb, u, vcb, oc, g = refs
    xb = x_ref[...].astype(w_ref.dtype)

    def proj(lo, width):
        return _mm(xb, w_ref[:, lo:lo + width])

    qa[...] = (proj(0, DA) * (HDA ** -0.5)).astype(qa.dtype)
    for f32_ref, bf_ref, lo in ((ka, kab, DA), (va, vab, 2 * DA)):
        t = proj(lo, DA)
        bf_ref[...] = t.astype(bf_ref.dtype)
        if n_stacked is None:
            f32_ref[...] = t
        else:
            f32_ref[n_stacked] = t.T
    if n_stacked:
        ka[0:n_stacked] = prev_k[...]
        va[0:n_stacked] = prev_v[...]
    for f32_ref, bf_ref, lo in ((kb, kbb, 3 * DA + DB), (vb, vbb, 3 * DA + 2 * DB)):
        t = proj(lo, DA)
        f32_ref[...] = t
        bf_ref[...] = t.astype(bf_ref.dtype)
    qb[...] = (proj(3 * DA, DB) * (HDB ** -0.5)).astype(qb.dtype)
    base = 3 * DA + 3 * DB
    for c in range(2):
        u[:, c * DC:(c + 1) * DC] = proj(base + c * DC, DC)
    vcb[...] = proj(base + 2 * DC, DC).astype(vcb.dtype)
    oc[...] = proj(base + 3 * DC, DC)
    g[...] = proj(Z_MAIN, LANES)


def _in_proj(x2d, w_bf16, tm, kv_stack=None):
    n = x2d.shape[0]
    narrow = w_bf16.dtype
    widths_dtypes = [(DA, narrow), (DA, F32), (DA, F32), (DA, narrow), (DA, narrow),
                     (DB, narrow), (DB, F32), (DB, F32), (DB, narrow), (DB, narrow),
                     (2 * DC, F32), (DC, narrow), (DC, F32), (LANES, F32)]
    out_specs = [pl.BlockSpec((tm, w), lambda i: (i, 0)) for w, _ in widths_dtypes]
    out_shape = [jax.ShapeDtypeStruct((n, w), dt) for w, dt in widths_dtypes]
    in_specs = [pl.BlockSpec((tm, D_MODEL), lambda i: (i, 0)), pl.BlockSpec((D_MODEL, Z_COLS), lambda i: (0, 0))]
    operands = [x2d, w_bf16]
    n_stacked = None
    if kv_stack is not None:
        batch, prev_k, prev_v = kv_stack
        tiles = n // batch // tm
        n_stacked = 0 if prev_k is None else prev_k.shape[0]
        stack_spec = lambda layers: pl.BlockSpec((layers, None, DA, tm), lambda i: (0, i // tiles, 0, i % tiles))
        for idx in (1, 2):
            out_specs[idx] = stack_spec(n_stacked + 1)
            out_shape[idx] = jax.ShapeDtypeStruct((n_stacked + 1, batch, DA, tiles * tm), F32)
        if n_stacked:
            in_specs += [stack_spec(n_stacked)] * 2
            operands += [prev_k, prev_v]
    return pl.pallas_call(
        functools.partial(_in_proj_kernel, n_stacked=n_stacked),
        grid=(n // tm,),
        in_specs=in_specs,
        out_specs=out_specs,
        out_shape=out_shape,
        compiler_params=_cparams("parallel"),
        name="in_proj",
    )(*operands)


def _sb_kernel(q_ref, k_ref, v_ref, o_ref, carry_ref, acc_ref, *, past, bq, bk, n_masked):
    i = pl.program_id(1)
    qs = _stack_heads(q_ref[...], HA, HDA)
    jj = lax.broadcasted_iota(jnp.int32, (bk, 2 * bk), 0)
    ss = lax.broadcasted_iota(jnp.int32, (bk, 2 * bk), 1)
    csum = jnp.where((ss >= bk) | (jj > ss), 1.0, 0.0).astype(BF16)
    csum2 = jnp.concatenate([csum, csum], axis=0)
    kb_last = (past + (i + 1) * bq - 2) // bk

    def block(kb, masked, first):
        start = pl.multiple_of(kb * bk, bk)
        s = _mm(qs, k_ref[pl.ds(start, bk), :], "nt")
        cost = jnp.maximum(s, 0.0) + jnp.log(1.0 + jnp.exp(-jnp.abs(s)))
        if masked:
            row = lax.broadcasted_iota(jnp.int32, (HA * bq, bk), 0) & (bq - 1)
            mask = (start + lax.broadcasted_iota(jnp.int32, (HA * bq, bk), 1)) < past + i * bq + row
            stay = jnp.where(mask, cost, 0.0)
        else:
            stay = cost
        hi = stay.astype(BF16)
        lo = (stay - hi.astype(F32)).astype(BF16)
        cs = _dot(jnp.concatenate([hi, lo], axis=1), csum2)
        before = cs[:, :bk] if first else carry_ref[...] + cs[:, :bk]
        w = jnp.exp(s - cost - before)
        if masked:
            w = jnp.where(mask, w, 0.0)
        pv = _mm(w.astype(v_ref.dtype), v_ref[pl.ds(start, bk), :])
        if first:
            carry = cs[:, bk:]
            acc_ref[...] = pv
        else:
            carry = carry_ref[...] + cs[:, bk:]
            acc_ref[...] += pv
        carry_ref[...] = carry
        return (jnp.min(carry) > SB_STAY_CUTOFF).astype(jnp.int32)

    done0 = block(kb_last, True, True)
    for d in range(1, n_masked):
        done0 = block(kb_last - d, True, False)

    def cond(state):
        kb, done = state
        return jnp.logical_and(kb >= 0, done == 0)

    def body(state):
        kb, _ = state
        return kb - 1, block(kb, False, False)

    lax.while_loop(cond, body, (kb_last - n_masked, done0))
    o_ref[...] = _unstack_heads(acc_ref[...], HA, HDA).astype(o_ref.dtype)


def _sb_attention(q_bf16, k_bf16, v_bf16, *, batch, tq, past, bq, out_dtype):
    bk = SB_BLOCK
    nq = tq // bq
    tk = k_bf16.shape[1]
    assert tk % bk == 0 and (past + tq - 2) // bk < tk // bk
    assert past % bk == 0 and (bq % bk == 0 or (nq == 1 and bq <= bk))
    n_masked = max(bq // bk, 1)
    return pl.pallas_call(
        functools.partial(_sb_kernel, past=past, bq=bq, bk=bk, n_masked=n_masked),
        grid=(batch, nq),
        in_specs=[pl.BlockSpec((bq, DA), lambda b, i: (b * nq + i, 0)),
                  pl.BlockSpec((None, tk, DA), lambda b, i: (b, 0, 0)),
                  pl.BlockSpec((None, tk, DA), lambda b, i: (b, 0, 0))],
        out_specs=pl.BlockSpec((bq, DA), lambda b, i: (b * nq + i, 0)),
        out_shape=jax.ShapeDtypeStruct((batch * tq, DA), out_dtype),
        scratch_shapes=[pltpu.VMEM((HA * bq, bk), F32), pltpu.VMEM((HA * bq, DA), F32)],
        compiler_params=_cparams("parallel", "arbitrary"),
        name="sb_attention",
    )(q_bf16, k_bf16, v_bf16)


def _band_kernel(q_ref, *refs, n_kb, rows_per_pass):
    k_refs, v_refs = refs[:n_kb], refs[n_kb:2 * n_kb]
    bias_ref, o_ref, s_ref, p_ref = refs[2 * n_kb:]
    j = pl.program_id(1)
    qs = _stack_heads(q_ref[...], HB, HDB)
    kbs = k_refs[0].shape[0]
    for i in range(n_kb):
        s = _mm(qs, k_refs[i][...], "nt") + bias_ref[:, i * kbs:(i + 1) * kbs]
        if i < n_kb - 1:
            s = s + jnp.where(j + i < n_kb - 1, -jnp.inf, 0.0)
        s_ref[:, i * kbs:(i + 1) * kbs] = s
    half = s_ref.shape[0] // 2
    accs = []
    for h0 in (0, half):
        for r in range(h0, h0 + half, rows_per_pass):
            s = s_ref[r:r + rows_per_pass, :]
            e = jnp.exp(s - jnp.max(s, axis=-1, keepdims=True))
            p_ref[r:r + rows_per_pass, :] = (e * (1.0 / jnp.sum(e, axis=-1, keepdims=True))).astype(p_ref.dtype)
        acc = _mm(p_ref[h0:h0 + half, 0:kbs], v_refs[0][...])
        for i in range(1, n_kb):
            acc = acc + _mm(p_ref[h0:h0 + half, i * kbs:(i + 1) * kbs], v_refs[i][...])
        accs.append(acc)
    o_ref[...] = _unstack_heads(jnp.concatenate(accs, axis=0), HB, HDB).astype(o_ref.dtype)


def _band_attention(q_bf16, k_bf16, v_bf16, bias, *, batch, steps, gq, n_kb, kbs, out_dtype):
    blocks_per_batch = k_bf16.shape[0] // (batch * kbs)
    kv_specs = [pl.BlockSpec((kbs, DB), functools.partial(
        lambda b, j, i: (b * blocks_per_batch + jnp.maximum(j + i - (n_kb - 1), 0), 0), i=i)) for i in range(n_kb)]
    return pl.pallas_call(
        functools.partial(_band_kernel, n_kb=n_kb, rows_per_pass=min(128, HB * gq)),
        grid=(batch, steps),
        in_specs=[pl.BlockSpec((gq, DB), lambda b, j: (b * steps + j, 0))] + kv_specs + kv_specs
        + [pl.BlockSpec((HB * gq, n_kb * kbs), lambda b, j: (0, 0))],
        out_specs=pl.BlockSpec((gq, DB), lambda b, j: (b * steps + j, 0)),
        out_shape=jax.ShapeDtypeStruct((batch * steps * gq, DB), out_dtype),
        scratch_shapes=[pltpu.VMEM((HB * gq, n_kb * kbs), F32), pltpu.VMEM((HB * gq, n_kb * kbs), v_bf16.dtype)],
        compiler_params=_cparams("parallel", "arbitrary"),
        name="band_attention",
    )(q_bf16, *([k_bf16] * n_kb), *([v_bf16] * n_kb), bias)


def _mlstm_kernel(u_ref, vc_ref, oc_ref, g_ref, convw_ref, convb_ref, bif_ref, ng_ref,
                  cn0_ref, m0_ref, conv0_ref, o_ref, cn_ref, m_ref, ubuf, *, length):
    L = length
    c = pl.program_id(1)

    @pl.when(c == 0)
    def _():
        cn_ref[...] = cn0_ref[...]
        m_ref[...] = m0_ref[...]
        ubuf[0:8, :] = conv0_ref[...]

    ubuf[8:8 + L, :] = u_ref[...]
    y = convb_ref[...]
    for w in range(CONV_W):
        y = y + ubuf[8 - (CONV_W - 1) + w:8 - (CONV_W - 1) + w + L, :] * convw_ref[w:w + 1, :]
    tail = ubuf[L:L + 8, :]
    ubuf[0:8, :] = tail
    qk = y * jax.nn.sigmoid(y)

    g = g_ref[...] + bif_ref[...]
    lf_all, _ = _log_sigmoid_pair(g)
    r_i = lax.broadcasted_iota(jnp.int32, (L, L), 0)
    c_i = lax.broadcasted_iota(jnp.int32, (L, L), 1)
    causal = c_i <= r_i
    tril = jnp.where(causal, 1.0, 0.0).astype(BF16)
    hi, mid, lo = _split3(lf_all)
    b_all = _dot(tril, hi) + _dot(tril, mid) + _dot(tril, lo)
    lane = lax.broadcasted_iota(jnp.int32, (L, LANES), 1)
    mixed = jnp.where(lane < HC, g, b_all)
    eye = jnp.where(lax.broadcasted_iota(jnp.int32, (16, LANES), 0)
                    == lax.broadcasted_iota(jnp.int32, (16, LANES), 1), 1.0, 0.0).astype(BF16)
    hi, mid, lo = _split3(mixed)
    rows = _dot_nt(eye, hi) + _dot_nt(eye, mid) + _dot_nt(eye, lo)
    narrow = vc_ref.dtype
    ones_col = jnp.where(lax.broadcasted_iota(jnp.int32, (L, HDC), 1) == 0, 1.0, 0.0).astype(narrow)

    for h in range(HC):
        li_row = rows[h:h + 1, :]
        b_row = rows[HC + h:HC + h + 1, :]
        li_col = g[:, h:h + 1]
        b_col = b_all[:, HC + h:HC + h + 1]
        dmat = jnp.where(causal, b_col - b_row + li_row, -jnp.inf)
        m_prev = m_ref[h][0:1, 0:1]
        inter = b_col + m_prev
        m_t = jnp.maximum(inter, jnp.max(dmat, axis=-1, keepdims=True))
        w_intra = jnp.exp(dmat - m_t)
        w_inter = jnp.exp(inter - m_t)
        qh = qk[:, h * HDC:(h + 1) * HDC].astype(narrow)
        kf = qk[:, DC + h * HDC:DC + (h + 1) * HDC] * (HDC ** -0.5)
        vaug = jnp.concatenate([vc_ref[:, h * HDC:(h + 1) * HDC], ones_col], axis=1)
        s = _mm(qh, kf.astype(narrow), "nt") * w_intra
        cn = cn_ref[h]
        qcn = _mm(qh, cn.astype(narrow))
        sv = _mm(s.astype(narrow), vaug[:, :HDC])
        num = w_inter * qcn[:, :HDC] + sv
        den = w_inter * qcn[:, HDC:HDC + 1] + jnp.sum(s, axis=-1, keepdims=True)
        hh = num / jnp.maximum(jnp.abs(den), jnp.exp(-m_t))
        m_new = m_t[L - 1:L, :]
        decay = jnp.exp(inter[L - 1:L, :] - m_new)
        w_k = jnp.exp(b_col[L - 1:L, :] - b_col + li_col - m_new)
        cn_ref[h] = decay * cn + _mm((kf * w_k).astype(narrow), vaug, "tn")
        m_ref[h] = jnp.broadcast_to(m_new, (8, LANES))
        mu = jnp.mean(hh, axis=-1, keepdims=True)
        dev = hh - mu
        var = jnp.mean(dev * dev, axis=-1, keepdims=True)
        hn = dev * lax.rsqrt(var + LN_EPS) * ng_ref[:, h * HDC:(h + 1) * HDC]
        gate = jax.nn.sigmoid(oc_ref[:, h * HDC:(h + 1) * HDC])
        o_ref[:, h * HDC:(h + 1) * HDC] = (gate * hn).astype(o_ref.dtype)


def _mlstm(u, vc_bf16, oc, gates, conv_w, conv_b, bif_pad, norm_g, cn0, m0, conv0, *, batch, t, length,
           out_dtype):
    nc = t // length
    row = lambda w: pl.BlockSpec((length, w), lambda b, c: (b * nc + c, 0))
    const = lambda shape: pl.BlockSpec(shape, lambda b, c: (0,) * len(shape))
    per_batch = lambda shape: pl.BlockSpec((None,) + shape, lambda b, c: (b,) + (0,) * len(shape))
    return pl.pallas_call(
        functools.partial(_mlstm_kernel, length=length),
        grid=(batch, nc),
        in_specs=[row(2 * DC), row(DC), row(DC), row(LANES),
                  const((CONV_W, 2 * DC)), const((1, 2 * DC)), const((1, LANES)), const((1, DC)),
                  per_batch((HC, HDC, 2 * HDC)), per_batch((HC, 8, LANES)), per_batch((8, 2 * DC))],
        out_specs=[row(DC), per_batch((HC, HDC, 2 * HDC)), per_batch((HC, 8, LANES))],
        out_shape=[jax.ShapeDtypeStruct((batch * t, DC), out_dtype),
                   jax.ShapeDtypeStruct((batch, HC, HDC, 2 * HDC), F32),
                   jax.ShapeDtypeStruct((batch, HC, 8, LANES), F32)],
        scratch_shapes=[pltpu.VMEM((8 + length, 2 * DC), F32)],
        compiler_params=_cparams("parallel", "arbitrary"),
        name="mlstm",
    )(u, vc_bf16, oc, gates, conv_w, conv_b, bif_pad, norm_g, cn0, m0, conv0)


def _layer_norm(y, g, b):
    mu = jnp.mean(y, axis=-1, keepdims=True)
    dev = y - mu
    var = jnp.mean(dev * dev, axis=-1, keepdims=True)
    return dev * lax.rsqrt(var + LN_EPS) * g + b


def _out_proj_kernel(x_ref, a_ref, b_ref, c_ref, w_ref, g_ref, beta_ref, o_ref):
    half = x_ref.shape[0] // 2
    for h0 in (0, half):
        rows = slice(h0, h0 + half)
        mix = (_mm(a_ref[rows, :], w_ref[0:DA, :]) + _mm(b_ref[rows, :], w_ref[DA:DA + DB, :])
               + _mm(c_ref[rows, :], w_ref[DA + DB:, :]))
        o_ref[rows, :] = _layer_norm(ALPHA * x_ref[rows, :] + mix, g_ref[...], beta_ref[...])


def _out_proj(x2d, a, b, c, w, g, beta, tm):
    assert a.dtype == b.dtype == c.dtype == w.dtype
    n = x2d.shape[0]
    row = lambda w: pl.BlockSpec((tm, w), lambda i: (i, 0))
    const = lambda shape: pl.BlockSpec(shape, lambda i: (0, 0))
    return pl.pallas_call(
        _out_proj_kernel,
        grid=(n // tm,),
        in_specs=[row(D_MODEL), row(DA), row(DB), row(DC), const((D_MODEL, D_MODEL)),
                  const((1, D_MODEL)), const((1, D_MODEL))],
        out_specs=row(D_MODEL),
        out_shape=jax.ShapeDtypeStruct((n, D_MODEL), F32),
        compiler_params=_cparams("parallel"),
        name="out_proj_ln",
    )(x2d, a, b, c, w, g, beta)


def _route(logits):
    lane = lax.broadcasted_iota(jnp.int32, logits.shape, 1)
    big = jnp.int32(LANES)
    is_g = lane < N_GROUPS
    lg = jnp.where(is_g, logits, -jnp.inf)
    g_max = jnp.max(lg, axis=-1, keepdims=True)
    g_sel = jnp.min(jnp.where(lg == g_max, lane, big), axis=-1, keepdims=True)
    p_g = 1.0 / jnp.sum(jnp.where(is_g, jnp.exp(logits - g_max), 0.0), axis=-1, keepdims=True)
    e_lo = N_GROUPS + g_sel * E_PER_GROUP
    in_group = (lane >= e_lo) & (lane < e_lo + E_PER_GROUP)
    le = jnp.where(in_group, logits, -jnp.inf)
    v1 = jnp.max(le, axis=-1, keepdims=True)
    i1 = jnp.min(jnp.where(le == v1, lane, big), axis=-1, keepdims=True)
    le2 = jnp.where(lane == i1, -jnp.inf, le)
    v2 = jnp.max(le2, axis=-1, keepdims=True)
    i2 = jnp.min(jnp.where(le2 == v2, lane, big), axis=-1, keepdims=True)
    e2 = jnp.exp(v2 - v1)
    tot = 1.0 + e2
    comb = jnp.where(lane == i1, p_g * (1.0 / tot), 0.0) + jnp.where(lane == i2, p_g * (e2 / tot), 0.0)
    return comb, g_sel


def _moe_kernel(x_ref, rw_ref, rb_ref, wg_ref, wu_ref, wd_ref, g_ref, beta_ref, o_ref,
                xs_ref, combs_ref, acc_ref, pos_ref, tab_ref):
    step = pl.program_id(1)
    tm = x_ref.shape[0]
    n_sorted = xs_ref.shape[0]
    blk = MOE_BLOCK

    @pl.when(step == 0)
    def _():
        xb = x_ref[...].astype(BF16)
        logits = _mm(x_ref[...].astype(rw_ref.dtype), rw_ref[...])
        comb, g_sel = _route(logits + rb_ref[...])
        lane = lax.broadcasted_iota(jnp.int32, (tm, LANES), 1)
        onehot = jnp.where(lane == g_sel, 1.0, 0.0)
        onehot_bf = onehot.astype(BF16)
        col = lax.broadcasted_iota(jnp.int32, (blk, tm), 1)
        row = lax.broadcasted_iota(jnp.int32, (blk, tm), 0)
        cum = jnp.concatenate(
            [_dot(jnp.where(col <= row + r, 1.0, 0.0).astype(BF16), onehot_bf) for r in range(0, tm, blk)], axis=0)
        count = cum[tm - 1:tm, :]
        padded = jnp.floor((count + (blk - 1)) * (1.0 / blk)) * blk
        lane1 = lax.broadcasted_iota(jnp.int32, (1, LANES), 1)
        offsets = jnp.zeros((1, LANES), F32)
        for grp in range(N_GROUPS):
            off = jnp.sum(jnp.where(lane1 < grp, padded, 0.0), axis=-1, keepdims=True)
            n_blocks = jnp.sum(jnp.where(lane1 == grp, padded, 0.0), axis=-1, keepdims=True) * (1.0 / blk)
            offsets = jnp.where(lane1 == grp, off, offsets)
            tab_ref[grp] = off.astype(jnp.int32)[0, 0]
            tab_ref[N_GROUPS + grp] = n_blocks.astype(jnp.int32)[0, 0]
        pos = jnp.sum(onehot * (offsets + cum), axis=-1, keepdims=True) - 1.0
        pos_ref[...] = pos.astype(jnp.int32)
        pos_hi = jnp.floor(pos * (1.0 / 256.0))
        digits = jnp.where(lane == 0, pos_hi, jnp.where(lane == 1, pos - 256.0 * pos_hi, 0.0)).astype(BF16)
        eye = jnp.where(lax.broadcasted_iota(jnp.int32, (16, LANES), 0)
                        == lax.broadcasted_iota(jnp.int32, (16, LANES), 1), 1.0, 0.0).astype(BF16)
        dig_rows = _dot_nt(eye, digits)
        pos_row = (dig_rows[0:1, :] * 256.0 + dig_rows[1:2, :]).astype(jnp.int32)
        c_hi = comb.astype(BF16)
        c_lo = (comb - c_hi.astype(F32)).astype(BF16)
        payload = jnp.concatenate([xb, c_hi, c_lo], axis=1)
        for r in range(0, n_sorted, blk):
            perm = jnp.where(row == pos_row - r, 1.0, 0.0).astype(BF16)
            moved = _dot(perm, payload)
            xs_ref[r:r + blk, :] = moved[:, :D_MODEL].astype(BF16)
            combs_ref[r:r + blk, :] = moved[:, D_MODEL:D_MODEL + LANES] + moved[:, D_MODEL + LANES:]
        acc_ref[...] = jnp.zeros_like(acc_ref)

    grp = (step * MOE_EXPERTS_PER_STEP) // E_PER_GROUP
    start = tab_ref[grp]
    n_blocks = tab_ref[N_GROUPS + grp]

    def expert_rows(r0, rows):
        xsb = xs_ref[pl.ds(r0, rows), :]
        combs = combs_ref[pl.ds(r0, rows), :]
        acc = acc_ref[pl.ds(r0, rows), :]
        lane_b = lax.broadcasted_iota(jnp.int32, (rows, LANES), 1)
        for j in range(MOE_EXPERTS_PER_STEP):
            gate = _dot(xsb, wg_ref[j])
            h = gate * jax.nn.sigmoid(gate) * _dot(xsb, wu_ref[j])
            y = _dot(h.astype(BF16), wd_ref[j])
            lane_e = N_GROUPS + step * MOE_EXPERTS_PER_STEP + j
            acc = acc + jnp.sum(jnp.where(lane_b == lane_e, combs, 0.0), axis=-1, keepdims=True) * y
        acc_ref[pl.ds(r0, rows), :] = acc

    def block_pair(b, carry):
        expert_rows(pl.multiple_of(start + b * (2 * blk), blk), 2 * blk)
        return carry

    lax.fori_loop(0, n_blocks // 2, block_pair, 0)

    @pl.when(n_blocks % 2 == 1)
    def _():
        expert_rows(pl.multiple_of(start + (n_blocks - 1) * blk, blk), blk)

    @pl.when(step == N_EXPERTS // MOE_EXPERTS_PER_STEP - 1)
    def _():
        for r in range(0, n_sorted, blk):
            xs_ref[r:r + blk, :] = acc_ref[r:r + blk, :].astype(BF16)
        lane_s = lax.broadcasted_iota(jnp.int32, (blk, n_sorted), 1)
        for r in range(0, tm, blk):
            back = jnp.where(lane_s == pos_ref[r:r + blk, :], 1.0, 0.0).astype(BF16)
            y = _dot(back, xs_ref[...])
            o_ref[r:r + blk, :] = _layer_norm(ALPHA * x_ref[r:r + blk, :] + y, g_ref[...], beta_ref[...])


def _moe(x2d, rw, rb, wg, wu, wd, g, beta, tm, layer):
    n = x2d.shape[0]
    n_sorted = tm + N_GROUPS * MOE_BLOCK
    per_step = MOE_EXPERTS_PER_STEP
    assert tm % MOE_BLOCK == 0 and n_sorted < 256 * 256 and E_PER_GROUP % per_step == 0
    const = lambda shape: pl.BlockSpec(shape, lambda i, e: (0, 0))
    return pl.pallas_call(
        _moe_kernel,
        grid=(n // tm, N_EXPERTS // per_step),
        in_specs=[pl.BlockSpec((tm, D_MODEL), lambda i, e: (i, 0)),
                  const((D_MODEL, LANES)), const((1, LANES)),
                  pl.BlockSpec((None, per_step, D_MODEL, D_EXPERT), lambda i, e: (layer, e, 0, 0)),
                  pl.BlockSpec((None, per_step, D_MODEL, D_EXPERT), lambda i, e: (layer, e, 0, 0)),
                  pl.BlockSpec((None, per_step, D_EXPERT, D_MODEL), lambda i, e: (layer, e, 0, 0)),
                  const((1, D_MODEL)), const((1, D_MODEL))],
        out_specs=pl.BlockSpec((tm, D_MODEL), lambda i, e: (i, 0)),
        out_shape=jax.ShapeDtypeStruct((n, D_MODEL), F32),
        scratch_shapes=[pltpu.VMEM((n_sorted, D_MODEL), BF16),
                        pltpu.VMEM((n_sorted, LANES), F32), pltpu.VMEM((n_sorted, D_MODEL), F32),
                        pltpu.VMEM((tm, 1), jnp.int32), pltpu.SMEM((2 * N_GROUPS,), jnp.int32)],
        compiler_params=_cparams("parallel", "arbitrary"),
        name="moe_ln",
    )(x2d, rw, rb, wg, wu, wd, g, beta)


def _band_bias(rel_table):
    n_clipped = BAND_W - 1 - REL_MAX
    ext = jnp.concatenate([rel_table[:, REL_MAX - (CHUNK - 1):],
                           jnp.broadcast_to(rel_table[:, -1:], (HB, n_clipped))], axis=1)
    n = ext.shape[1]
    assert n - 1 >= BAND_W and BAND_W - 1 + CHUNK - 1 < n
    v = jnp.roll(ext[:, ::-1], -(CHUNK - 1), axis=1)
    skewed = jnp.tile(v, (1, CHUNK))[:, :CHUNK * (n - 1)].reshape(HB, CHUNK, n - 1)
    return skewed[:, :, :BAND_W].astype(F32)


def _band_bias_grouped(bias):
    g_n = BAND_GROUP
    tabs = [jnp.pad(bias, ((0, 0), (0, 0), (g * CHUNK, (g_n - 1 - g) * CHUNK)), constant_values=-jnp.inf)
            for g in range(g_n)]
    return jnp.stack(tabs, axis=1).reshape(HB * g_n * CHUNK, (g_n + BAND_CHUNKS) * CHUNK)


def _layer_weights(l, w_in, b_if, conv_w, conv_b, rel_table, c_norm_g, w_out, ln1_g, ln1_b,
                   router_g_w, router_g_b, router_e_w, router_e_b, exp_w_gate, exp_w_up, exp_w_down,
                   ln2_g, ln2_b):
    pad_cols = lambda a: jnp.pad(a, ((0, 0), (0, LANES - a.shape[1])))
    bias = _band_bias(rel_table[l])
    return dict(
        w_in=jnp.concatenate([w_in[l, :, :Z_MAIN], pad_cols(w_in[l, :, Z_MAIN:])], axis=1).astype(BF16),
        w_in_f32=jnp.concatenate([w_in[l, :, :Z_MAIN], pad_cols(w_in[l, :, Z_MAIN:])], axis=1),
        bif=pad_cols(b_if[l][None, :]),
        conv_w=conv_w[l], conv_b=conv_b[l][None, :],
        bias=bias.reshape(HB * CHUNK, BAND_W), bias_grouped=_band_bias_grouped(bias),
        norm_g=c_norm_g[l].reshape(1, DC),
        w_out=w_out[l].astype(BF16), w_out_f32=w_out[l],
        ln1_g=ln1_g[l][None, :], ln1_b=ln1_b[l][None, :],
        rw=pad_cols(jnp.concatenate([router_g_w[l], router_e_w[l]], axis=1)).astype(BF16),
        rw_f32=pad_cols(jnp.concatenate([router_g_w[l], router_e_w[l]], axis=1)),
        rb=pad_cols(jnp.concatenate([router_g_b[l], router_e_b[l]])[None, :]),
        wg=exp_w_gate, wu=exp_w_up, wd=exp_w_down, layer=l,
        ln2_g=ln2_g[l][None, :], ln2_b=ln2_b[l][None, :],
    )


def _layer(x, w, cache, *, tm, tm_moe, sb_bq, length, band_keep, kv_prev=(None, None), split_products=False):
    batch, t, _ = x.shape
    n = batch * t
    tm, tm_moe = min(tm, n), min(tm_moe, n)
    x2d = x.reshape(n, D_MODEL)
    narrow = F32 if split_products else BF16
    pick = (lambda name: w[name + "_f32"]) if split_products else (lambda name: w[name])
    qa, ka, va, ka_bf, va_bf, qb, kb, vb, kb_bf, vb_bf, u, vc_bf, oc, gates = _in_proj(
        x2d, pick("w_in"), tm, kv_stack=(batch, *kv_prev) if cache is None else None)
    per_batch = lambda a: a.reshape(batch, t, a.shape[-1])

    if cache is None:
        past = 0
        sb_k, sb_v = per_batch(ka_bf), per_batch(va_bf)
        gq = BAND_GROUP * CHUNK
        assert t % gq == 0 and BAND_CHUNKS * CHUNK % gq == 0
        band = dict(k=kb_bf, v=vb_bf, bias=w["bias_grouped"], steps=t // gq, gq=gq,
                    n_kb=BAND_CHUNKS * CHUNK // gq + 1, kbs=gq)
        cn0 = jnp.zeros((batch, HC, HDC, 2 * HDC), F32)
        m0 = jnp.zeros((batch, HC, 8, LANES), F32)
        conv0 = jnp.zeros((batch, 8, 2 * DC), F32)
    else:
        a_k, a_v, b_k, b_v, c0, n0, m_init, conv_init = cache
        past = a_k.shape[1]
        tk_pad = -(past + t) % SB_BLOCK
        sb_cat = lambda old, new: jnp.pad(
            jnp.concatenate([old.reshape(batch, past, DA).astype(narrow), per_batch(new)], axis=1),
            ((0, 0), (0, tk_pad), (0, 0)))
        sb_k, sb_v = sb_cat(a_k, ka_bf), sb_cat(a_v, va_bf)
        hist = b_k.shape[1]
        assert hist == BAND_CHUNKS * CHUNK and t == CHUNK and past % CHUNK == 0 and past >= hist
        band_cat = lambda old, new: jnp.concatenate(
            [old.reshape(batch, hist, DB).astype(narrow), per_batch(new)], axis=1).reshape(batch * BAND_W, DB)
        band = dict(k=band_cat(b_k, kb_bf), v=band_cat(b_v, vb_bf), bias=w["bias"], steps=1, gq=CHUNK,
                    n_kb=1, kbs=BAND_W)
        cn0 = jnp.concatenate([c0, n0[..., None], jnp.zeros((batch, HC, HDC, HDC - 1), F32)], axis=-1)
        m0 = jnp.broadcast_to(m_init[:, :, None, None], (batch, HC, 8, LANES))
        conv0 = jnp.pad(conv_init, ((0, 0), (8 - (CONV_W - 1), 0), (0, 0)))

    a_out = _sb_attention(qa, sb_k, sb_v, batch=batch, tq=t, past=past, bq=sb_bq, out_dtype=narrow)
    b_out = _band_attention(qb, band["k"], band["v"], band["bias"], batch=batch, steps=band["steps"],
                            gq=band["gq"], n_kb=band["n_kb"], kbs=band["kbs"], out_dtype=narrow)
    c_out, cn1, m1 = _mlstm(u, vc_bf, oc, gates, w["conv_w"], w["conv_b"], w["bif"], w["norm_g"],
                            cn0, m0, conv0, batch=batch, t=t, length=length, out_dtype=narrow)
    x1 = _out_proj(x2d, a_out, b_out, c_out, pick("w_out"), w["ln1_g"], w["ln1_b"], tm)
    x2 = _moe(x1, pick("rw"), w["rb"], w["wg"], w["wu"], w["wd"], w["ln2_g"], w["ln2_b"], tm_moe, w["layer"])
    heads = (lambda a: a) if cache is None else (lambda a: a.reshape(batch, t, HA, HDA))
    state = (heads(ka), heads(va),
             per_batch(kb)[:, t - band_keep:].reshape(batch, band_keep, HB, HDB),
             per_batch(vb)[:, t - band_keep:].reshape(batch, band_keep, HB, HDB),
             cn1[..., :HDC], cn1[..., HDC], m1[:, :, 0, 0], per_batch(u)[:, t - (CONV_W - 1):])
    return x2.reshape(batch, t, D_MODEL), state


def kernel(x_prompt, x_sample, cache_a_k, cache_a_v, cache_b_k, cache_b_v, state_c_C, state_c_n,
           state_c_m, state_c_conv, w_in, b_if, conv_w, conv_b, rel_table, c_norm_g, w_out, ln1_g, ln1_b,
           router_g_w, router_g_b, router_e_w, router_e_b, exp_w_gate, exp_w_up, exp_w_down, ln2_g, ln2_b):
    band_keep = cache_b_k.shape[2]
    xp, xs = x_prompt, x_sample
    prompt_states, sample_states = [], []
    kv_stack = (None, None)
    experts_bf16 = tuple(a.astype(BF16) for a in (exp_w_gate, exp_w_up, exp_w_down))
    for l in range(DEPTH):
        w = _layer_weights(l, w_in, b_if, conv_w, conv_b, rel_table, c_norm_g, w_out, ln1_g, ln1_b,
                           router_g_w, router_g_b, router_e_w, router_e_b, *experts_bf16, ln2_g, ln2_b)
        xp, sp = _layer(xp, w, None, tm=512, tm_moe=1024, sb_bq=2 * SB_BLOCK, length=2 * CHUNK,
                        band_keep=min(band_keep, xp.shape[1]), kv_prev=kv_stack)
        kv_stack = sp[:2]
        prompt_states.append(sp[2:])
        cache_l = (cache_a_k[l], cache_a_v[l], cache_b_k[l], cache_b_v[l],
                   state_c_C[l], state_c_n[l], state_c_m[l], state_c_conv[l])
        xs, ss = _layer(xs, w, cache_l, tm=512, tm_moe=512, sb_bq=CHUNK, length=CHUNK,
                        band_keep=xs.shape[1], split_products=l < DEPTH - 1)
        sample_states.append(ss)
    p_kv = [a.reshape(DEPTH, a.shape[1], HA, HDA, a.shape[3]).transpose(0, 1, 4, 2, 3) for a in kv_stack]
    p = [jnp.stack(s) for s in zip(*prompt_states)]
    s = [jnp.stack(s) for s in zip(*sample_states)]
    return (xp, xs, *p_kv, *p, *s)
```
